```python
import jax, jax.numpy as jnp
from jax import lax
import numpy as np

D_MODEL = 1024
BATCH = 8
SEQ = 2048
DEPTH = 2

N_MIXERS = 2
N_MLA_LAYERS = (DEPTH + 1) // 2
N_SB_LAYERS = DEPTH // 2
BLOCK_Q = 128
EPS = 1e-6
MLA_HEADS = 8
MLA_Q_LORA = 384
MLA_KV_LORA = 256
MLA_NOPE = 128
MLA_ROPE = 64
MLA_V = 128
ROPE_THETA = 10000.0
POS_OFFSET_MAX = 4096
SB_HEADS = 8
SB_HEAD_DIM = D_MODEL // SB_HEADS
MEM_LEN = 256
MEM_HEADS = 4
MEM_HEAD_DIM = 128
D_FF = 4 * D_MODEL

kernel_name = "hybrid_mla_stickbreaking_memxattn_sqrelu"

NEG_BIG = -1e30


def rmsnorm(x, g):
    xf = x.astype(jnp.float32)
    y = xf * lax.rsqrt(jnp.mean(xf * xf, axis=-1, keepdims=True) + EPS)
    return (y * g.astype(jnp.float32)).astype(x.dtype)


def rope_tables(positions):
    inv_freq = ROPE_THETA ** (-jnp.arange(0, MLA_ROPE, 2, dtype=jnp.float32) / MLA_ROPE)
    ang = positions.astype(jnp.float32)[..., None] * inv_freq
    return jnp.cos(ang)[:, :, None, :], jnp.sin(ang)[:, :, None, :]


def apply_rope(x, cos, sin):
    half = x.shape[-1] // 2
    x1, x2 = x[..., :half], x[..., half:]
    c = cos.astype(x.dtype)
    s = sin.astype(x.dtype)
    return jnp.concatenate([x1 * c - x2 * s, x1 * s + x2 * c], axis=-1)


def blocked_causal_softmax_attn(q, k, v, scale):
    S = q.shape[1]
    outs = []
    for i in range(S // BLOCK_Q):
        lo, hi = i * BLOCK_Q, (i + 1) * BLOCK_Q
        s = jnp.einsum('bqhd,bkhd->bhqk', q[:, lo:hi], k[:, :hi]).astype(jnp.float32) * scale
        causal = (lo + jnp.arange(BLOCK_Q))[:, None] >= jnp.arange(hi)[None, :]
        p = jax.nn.softmax(jnp.where(causal, s, NEG_BIG), axis=-1).astype(v.dtype)
        outs.append(jnp.einsum('bhqk,bkhd->bqhd', p, v[:, :hi]))
    return jnp.concatenate(outs, axis=1)


def mla_mixer(h, cos, sin, w_dkv, g_q, g_kv, w_uq, w_ukv, w_o):
    B, S, _ = h.shape
    lat = h @ w_dkv
    c_q = rmsnorm(lat[..., :MLA_Q_LORA], g_q)
    c_kv = rmsnorm(lat[..., MLA_Q_LORA:MLA_Q_LORA + MLA_KV_LORA], g_kv)
    k_pe = lat[..., MLA_Q_LORA + MLA_KV_LORA:][:, :, None, :]
    q = (c_q @ w_uq).reshape(B, S, MLA_HEADS, MLA_NOPE + MLA_ROPE)
    kv = (c_kv @ w_ukv).reshape(B, S, MLA_HEADS, MLA_NOPE + MLA_V)
    q_pe = apply_rope(q[..., MLA_NOPE:], cos, sin)
    k_pe = apply_rope(k_pe, cos, sin)
    q = jnp.concatenate([q[..., :MLA_NOPE], q_pe], axis=-1)
    k = jnp.concatenate([kv[..., :MLA_NOPE],
                         jnp.broadcast_to(k_pe, (B, S, MLA_HEADS, MLA_ROPE))], axis=-1)
    v = kv[..., MLA_NOPE:]
    o = blocked_causal_softmax_attn(q, k, v, (MLA_NOPE + MLA_ROPE) ** -0.5)
    return o.reshape(B, S, MLA_HEADS * MLA_V) @ w_o


def stick_breaking_mixer(h, w_qkv, w_o):
    B, S, _ = h.shape
    qkv = (h @ w_qkv).reshape(B, S, 3, SB_HEADS, SB_HEAD_DIM)
    q, k, v = qkv[:, :, 0], qkv[:, :, 1], qkv[:, :, 2]
    scale = SB_HEAD_DIM ** -0.5
    outs = []
    for i in range(S // BLOCK_Q):
        lo, hi = i * BLOCK_Q, (i + 1) * BLOCK_Q
        z = jnp.einsum('bqhd,bkhd->bhqk', q[:, lo:hi], k[:, :hi]).astype(jnp.float32) * scale
        strict = jnp.arange(hi)[None, :] < (lo + jnp.arange(BLOCK_Q))[:, None]
        log_fail = jnp.where(strict, -jax.nn.softplus(z), 0.0)
        excl = lax.cumsum(log_fail, axis=3, reverse=True) - log_fail
        a = jnp.where(strict, jnp.exp(jax.nn.log_sigmoid(z) + excl), 0.0)
        outs.append(jnp.einsum('bhqk,bkhd->bqhd', a.astype(v.dtype), v[:, :hi]))
    o = jnp.concatenate(outs, axis=1)
    return o.reshape(B, S, SB_HEADS * SB_HEAD_DIM) @ w_o


def mem_cross_attn(h, m, w_q, w_kv, w_o):
    B, S, _ = h.shape
    q = (h @ w_q).reshape(B, S, MEM_HEADS, MEM_HEAD_DIM)
    kv = (m @ w_kv).reshape(B, m.shape[1], 2, MEM_HEADS, MEM_HEAD_DIM)
    s = jnp.einsum('bqhd,bmhd->bhqm', q, kv[:, :, 0]).astype(jnp.float32) * MEM_HEAD_DIM ** -0.5
    p = jax.nn.softmax(s, axis=-1).astype(h.dtype)
    o = jnp.einsum('bhqm,bmhd->bqhd', p, kv[:, :, 1])
    return o.reshape(B, S, MEM_HEADS * MEM_HEAD_DIM) @ w_o


def sq_relu_mlp(h, w_in, w_out):
    return jnp.square(jax.nn.relu(h @ w_in)) @ w_out


def setup_inputs(seed: int = 0) -> dict:
    key = jax.random.key(seed)
    ks = iter(jax.random.split(key, 32))
    f32 = jnp.float32

    def w(shape, fan_in):
        return jax.random.normal(next(ks), shape, f32) * (fan_in ** -0.5)

    def gain(shape):
        return 1.0 + 0.02 * jax.random.normal(next(ks), shape, f32)

    x = jax.random.normal(next(ks), (BATCH, SEQ, D_MODEL), f32)
    mem = jax.random.normal(next(ks), (BATCH, MEM_LEN, D_MODEL), f32)
    offset = jax.random.randint(next(ks), (BATCH, 1), 0, POS_OFFSET_MAX, dtype=jnp.int32)
    positions = (offset + jnp.arange(SEQ, dtype=jnp.int32)[None, :]).astype(jnp.int32)
    return {
        "x": x,
        "mem": mem,
        "positions": positions,
        "norm_mix": gain((DEPTH, D_MODEL)),
        "norm_cross": gain((DEPTH, D_MODEL)),
        "norm_mem": gain((DEPTH, D_MODEL)),
        "norm_mlp": gain((DEPTH, D_MODEL)),
        "norm_final": gain((D_MODEL,)),
        "mla_w_dkv": w((N_MLA_LAYERS, D_MODEL, MLA_Q_LORA + MLA_KV_LORA + MLA_ROPE), D_MODEL),
        "mla_g_q": gain((N_MLA_LAYERS, MLA_Q_LORA)),
        "mla_g_kv": gain((N_MLA_LAYERS, MLA_KV_LORA)),
        "mla_w_uq": w((N_MLA_LAYERS, MLA_Q_LORA, MLA_HEADS * (MLA_NOPE + MLA_ROPE)), MLA_Q_LORA),
        "mla_w_ukv": w((N_MLA_LAYERS, MLA_KV_LORA, MLA_HEADS * (MLA_NOPE + MLA_V)), MLA_KV_LORA),
        "mla_w_o": w((N_MLA_LAYERS, MLA_HEADS * MLA_V, D_MODEL), MLA_HEADS * MLA_V),
        "sb_w_qkv": w((N_SB_LAYERS, D_MODEL, 3 * SB_HEADS * SB_HEAD_DIM), D_MODEL),
        "sb_w_o": w((N_SB_LAYERS, SB_HEADS * SB_HEAD_DIM, D_MODEL), SB_HEADS * SB_HEAD_DIM),
        "xa_w_q": w((DEPTH, D_MODEL, MEM_HEADS * MEM_HEAD_DIM), D_MODEL),
        "xa_w_kv": w((DEPTH, D_MODEL, 2 * MEM_HEADS * MEM_HEAD_DIM), D_MODEL),
        "xa_w_o": w((DEPTH, MEM_HEADS * MEM_HEAD_DIM, D_MODEL), MEM_HEADS * MEM_HEAD_DIM),
        "mlp_w_in": w((DEPTH, D_MODEL, D_FF), D_MODEL),
        "mlp_w_out": w((DEPTH, D_FF, D_MODEL), D_FF),
    }


def reference(x, mem, positions, norm_mix, norm_cross, norm_mem, norm_mlp, norm_final,
              mla_w_dkv, mla_g_q, mla_g_kv, mla_w_uq, mla_w_ukv, mla_w_o,
              sb_w_qkv, sb_w_o, xa_w_q, xa_w_kv, xa_w_o, mlp_w_in, mlp_w_out):
    cos, sin = rope_tables(positions)
    h = x
    for i in range(DEPTH):
        a = rmsnorm(h, norm_mix[i])
        j = i // N_MIXERS
        if i % N_MIXERS == 0:
            h = h + mla_mixer(a, cos, sin, mla_w_dkv[j], mla_g_q[j], mla_g_kv[j],
                              mla_w_uq[j], mla_w_ukv[j], mla_w_o[j])
        else:
            h = h + stick_breaking_mixer(a, sb_w_qkv[j], sb_w_o[j])
        h = h + mem_cross_attn(rmsnorm(h, norm_cross[i]), rmsnorm(mem, norm_mem[i]),
                               xa_w_q[i], xa_w_kv[i], xa_w_o[i])
        h = h + sq_relu_mlp(rmsnorm(h, norm_mlp[i]), mlp_w_in[i], mlp_w_out[i])
    return rmsnorm(h, norm_final)
```

```python
import functools
import math

import jax
import jax.numpy as jnp
from jax import lax
from jax.experimental import pallas as pl
from jax.experimental.pallas import tpu as pltpu

F32 = jnp.float32
BF16 = jnp.bfloat16

D_MODEL = 1024
EPS = 1e-6
MLA_HEADS = 8
MLA_Q_LORA = 384
MLA_KV_LORA = 256
MLA_NOPE = 128
MLA_ROPE = 64
MLA_V = 128
ROPE_THETA = 10000.0
SB_HEADS = 8
SB_HEAD_DIM = 128
MEM_HEADS = 4
MEM_HEAD_DIM = 128
NEG_BIG = -1e30
LOG2E = math.log2(math.e)

LANES = 128
V7X_VMEM_BYTES = 64 * 1024 * 1024

TQ = 256
TK = 256
TM = 512
FF_CHUNK = 1024


def _vmem_limit(nbytes):
    return int(min(max(nbytes, 32 * 1024 * 1024), V7X_VMEM_BYTES - 8 * 1024 * 1024))


def _params(semantics, vmem_bytes):
    return pltpu.CompilerParams(dimension_semantics=semantics,
                                vmem_limit_bytes=_vmem_limit(vmem_bytes))


def _const_spec(shape):
    nd = len(shape)
    return pl.BlockSpec(shape, lambda *_: (0,) * nd)


def _rmsnorm(x, g):
    return x * lax.rsqrt(jnp.mean(x * x, axis=-1, keepdims=True) + EPS) * g


def _dot(a, b):
    return jnp.dot(a, b, preferred_element_type=F32)


def _dot_nt(a, b):
    return lax.dot_general(a, b, (((1,), (1,)), ((), ())), preferred_element_type=F32)


def _rope_kernel(pos_ref, freq_ref, cos_ref, sin_ref):
    ang = pos_ref[...].astype(F32) * freq_ref[...]
    cos_ref[...] = jnp.cos(ang)
    sin_ref[...] = jnp.sin(ang)


def _rope_tables(positions):
    n_tok = positions.size
    half = MLA_ROPE // 2
    inv_freq = ROPE_THETA ** (-jnp.arange(0, MLA_ROPE, 2, dtype=F32) / MLA_ROPE)
    per_row = LANES // half
    rows = n_tok // per_row
    pos_rep = jnp.repeat(positions.reshape(rows, per_row), half, axis=1)
    freq = jnp.tile(inv_freq, per_row).reshape(1, LANES)
    rb = 512
    cos, sin = pl.pallas_call(
        _rope_kernel,
        grid=(rows // rb,),
        in_specs=[pl.BlockSpec((rb, LANES), lambda i: (i, 0)), _const_spec((1, LANES))],
        out_specs=[pl.BlockSpec((rb, LANES), lambda i: (i, 0))] * 2,
        out_shape=[jax.ShapeDtypeStruct((rows, LANES), F32)] * 2,
        compiler_params=_params(("parallel",), 0),
        name="rope_tables",
    )(pos_rep, freq)
    cos = jnp.tile(cos.reshape(n_tok, half), (1, per_row))
    sin = jnp.tile(sin.reshape(n_tok, half), (1, per_row))
    return cos, sin


def _mem_kv_kernel(mem_ref, g_ref, w_ref, kv_ref):
    n = _rmsnorm(mem_ref[...], g_ref[...]).astype(BF16)
    kv_ref[...] = _dot(n, w_ref[...]).astype(BF16)


def _mem_kv(mem, norm_mem, w_kv):
    depth = norm_mem.shape[0]
    b, m, d = mem.shape
    n_out = w_kv.shape[-1]
    return pl.pallas_call(
        _mem_kv_kernel,
        grid=(depth, b),
        in_specs=[pl.BlockSpec((None, m, d), lambda l, i: (i, 0, 0)),
                  pl.BlockSpec((None, 1, d), lambda l, i: (l, 0, 0)),
                  pl.BlockSpec((None, d, n_out), lambda l, i: (l, 0, 0))],
        out_specs=pl.BlockSpec((None, None, m, n_out), lambda l, i: (l, i, 0, 0)),
        out_shape=jax.ShapeDtypeStruct((depth, b, m, n_out), BF16),
        compiler_params=_params(("parallel", "parallel"), 0),
        name="mem_kv",
    )(mem, norm_mem.reshape(depth, 1, d), w_kv)


MLA_QK_SCALE = (MLA_NOPE + MLA_ROPE) ** -0.5 * LOG2E
LAT_Q_END = MLA_Q_LORA
LAT_KV_END = MLA_Q_LORA + MLA_KV_LORA
LAT_EXT = LAT_KV_END + 2 * MLA_ROPE
Q_NOPE_W = MLA_HEADS * MLA_NOPE
Q_ROPE_W = MLA_HEADS * MLA_ROPE


def _mla_proj_kernel(x_ref, g_ref, wdkv_ref, gq_ref, gkv_ref, wuq_ref, wuk_ref, wvt_ref,
                     cos_ref, sin_ref, qn_ref, qp_ref, kn_ref, kpe_ref, vt_ref):
    a = _rmsnorm(x_ref[...], g_ref[...]).astype(BF16)
    lat = _dot(a, wdkv_ref[...])
    c_q = _rmsnorm(lat[:, :LAT_Q_END], gq_ref[...]).astype(BF16)
    c_kv = _rmsnorm(lat[:, LAT_Q_END:LAT_KV_END], gkv_ref[...]).astype(BF16)
    cos = cos_ref[...]
    sin = sin_ref[...]

    slab = lat[:, LAT_KV_END:LAT_EXT]
    rot = slab * cos + pltpu.roll(slab, MLA_ROPE, 1) * sin
    lane = lax.broadcasted_iota(jnp.int32, rot.shape, 1)
    k_even = jnp.where(lane < MLA_ROPE, rot, 0.0)
    k_odd = pltpu.roll(k_even, MLA_ROPE, 1)
    kpe_ref[...] = jnp.concatenate([k_even, k_odd], axis=1).astype(BF16)

    q = _dot(c_q, wuq_ref[...])
    qn_ref[...] = (q[:, :Q_NOPE_W] * MLA_QK_SCALE).astype(BF16)
    for s in range(Q_ROPE_W // LANES):
        lo = Q_NOPE_W + s * LANES
        r = q[:, lo:lo + LANES]
        r_partner = q[:, lo + Q_ROPE_W:lo + Q_ROPE_W + LANES]
        qp_ref[:, s * LANES:(s + 1) * LANES] = (
            (r * cos + r_partner * sin) * MLA_QK_SCALE).astype(BF16)

    kn_ref[...] = _dot(c_kv, wuk_ref[...]).astype(BF16)
    vt = _dot_nt(wvt_ref[...], c_kv)
    for c in range(vt_ref.shape[0]):
        vt_ref[c] = vt[:, c * TK:(c + 1) * TK].astype(BF16)


def _mla_proj(x2, g, w_dkv, g_q, g_kv, w_uq, w_ukv, cos, sin):
    t, d = x2.shape
    h = MLA_HEADS
    kpe0 = LAT_KV_END
    half = MLA_ROPE // 2
    wdkv = jnp.concatenate(
        [w_dkv, -w_dkv[:, kpe0 + half:kpe0 + MLA_ROPE], w_dkv[:, kpe0:kpe0 + half]],
        axis=1).astype(BF16)
    uq = w_uq.reshape(MLA_Q_LORA, h, MLA_NOPE + MLA_ROPE)
    uq_rope = uq[:, :, MLA_NOPE:]
    uq_partner = jnp.concatenate([-uq_rope[:, :, half:], uq_rope[:, :, :half]], axis=-1)
    wuq = jnp.concatenate(
        [uq[:, :, :MLA_NOPE].reshape(MLA_Q_LORA, Q_NOPE_W),
         uq_rope.reshape(MLA_Q_LORA, Q_ROPE_W),
         uq_partner.reshape(MLA_Q_LORA, Q_ROPE_W)], axis=1).astype(BF16)
    ukv = w_ukv.reshape(MLA_KV_LORA, h, MLA_NOPE + MLA_V)
    wuk = ukv[:, :, :MLA_NOPE].reshape(MLA_KV_LORA, h * MLA_NOPE).astype(BF16)
    wvt = ukv[:, :, MLA_NOPE:].reshape(MLA_KV_LORA, h * MLA_V).T.astype(BF16)

    tm = TM
    n_chunks = t // TK
    row = lambda i: (i, 0)
    out_shapes = [
        jax.ShapeDtypeStruct((t, Q_NOPE_W), BF16),
        jax.ShapeDtypeStruct((t, Q_ROPE_W), BF16),
        jax.ShapeDtypeStruct((t, h * MLA_NOPE), BF16),
        jax.ShapeDtypeStruct((t, 2 * LANES), BF16),
        jax.ShapeDtypeStruct((n_chunks, h * MLA_V, TK), BF16),
    ]
    return pl.pallas_call(
        _mla_proj_kernel,
        grid=(t // tm,),
        in_specs=[pl.BlockSpec((tm, d), row), _const_spec((1, d)),
                  _const_spec(wdkv.shape), _const_spec((1, MLA_Q_LORA)),
                  _const_spec((1, MLA_KV_LORA)), _const_spec(wuq.shape),
                  _const_spec(wuk.shape), _const_spec(wvt.shape),
                  pl.BlockSpec((tm, LANES), row), pl.BlockSpec((tm, LANES), row)],
        out_specs=[pl.BlockSpec((tm, Q_NOPE_W), row), pl.BlockSpec((tm, Q_ROPE_W), row),
                   pl.BlockSpec((tm, h * MLA_NOPE), row), pl.BlockSpec((tm, 2 * LANES), row),
                   pl.BlockSpec((tm // TK, h * MLA_V, TK), lambda i: (i, 0, 0))],
        out_shape=out_shapes,
        compiler_params=_params(("parallel",), 48 * 1024 * 1024),
        name="mla_proj",
    )(x2, g.reshape(1, d), wdkv, g_q.reshape(1, -1), g_kv.reshape(1, -1), wuq, wuk, wvt,
      cos, sin)


def _mla_attn_kernel(qn_ref, qp_ref, kn_ref, kpe_ref, vt_ref, o_ref, m_scr, l_scr, acc_scr):
    qi = pl.program_id(1)
    heads = MLA_HEADS
    hd = MLA_NOPE

    def chunk(kj, diag):
        koff = pl.multiple_of(kj * TK, TK)
        kpe = kpe_ref[pl.ds(koff, TK), :]
        if diag:
            key = lax.broadcasted_iota(jnp.int32, (TK, TQ), 0)
            qry = lax.broadcasted_iota(jnp.int32, (TK, TQ), 1)
            causal = key <= qry
        for h in range(heads):
            cols = slice(h * hd, (h + 1) * hd)
            pair = slice((h // 2) * LANES, (h // 2 + 1) * LANES)
            par = slice((h % 2) * LANES, (h % 2 + 1) * LANES)
            q_h = jnp.concatenate([qn_ref[:, cols], qp_ref[:, pair]], axis=1)
            k_h = jnp.concatenate([kn_ref[pl.ds(koff, TK), cols], kpe[:, par]], axis=1)
            s = _dot_nt(k_h, q_h)
            if diag:
                s = jnp.where(causal, s, NEG_BIG)
            m_cur = jnp.max(s, axis=0, keepdims=True)
            if diag:
                m_new = m_cur
            else:
                m_prev = m_scr[h:h + 1, :]
                m_new = jnp.maximum(m_prev, m_cur)
            p = jnp.exp2(s - m_new)
            p_sum = jnp.sum(p, axis=0, keepdims=True)
            pv = _dot(vt_ref[kj, cols, :], p.astype(BF16))
            if diag:
                l_scr[h:h + 1, :] = p_sum
                acc_scr[cols, :] = pv
            else:
                alpha = jnp.exp2(m_prev - m_new)
                l_scr[h:h + 1, :] = alpha * l_scr[h:h + 1, :] + p_sum
                acc_scr[cols, :] = alpha * acc_scr[cols, :] + pv
            m_scr[h:h + 1, :] = m_new

    chunk(qi, True)

    def body(kj, carry):
        chunk(kj, False)
        return carry

    lax.fori_loop(0, qi, body, 0)

    for h in range(heads):
        cols = slice(h * hd, (h + 1) * hd)
        o_t = acc_scr[cols, :] / l_scr[h:h + 1, :]
        o_ref[:, cols] = o_t.T.astype(BF16)


def _mla_attn(qn, qp, kn, kpe, vt, batch, seq):
    t = qn.shape[0]
    nq = seq // TQ
    nch = seq // TK
    hv = MLA_HEADS * MLA_V
    kn3 = kn.reshape(batch, seq, -1)
    kpe3 = kpe.reshape(batch, seq, -1)
    vt4 = vt.reshape(batch, nch, hv, TK)
    qrow = lambda b, i: (b * nq + i, 0)
    perb = lambda b, i: (b, 0, 0)
    return pl.pallas_call(
        _mla_attn_kernel,
        grid=(batch, nq),
        in_specs=[pl.BlockSpec((TQ, qn.shape[1]), qrow), pl.BlockSpec((TQ, qp.shape[1]), qrow),
                  pl.BlockSpec((None, seq, kn3.shape[2]), perb),
                  pl.BlockSpec((None, seq, kpe3.shape[2]), perb),
                  pl.BlockSpec((None, nch, hv, TK), lambda b, i: (b, 0, 0, 0))],
        out_specs=pl.BlockSpec((TQ, hv), qrow),
        out_shape=jax.ShapeDtypeStruct((t, hv), BF16),
        scratch_shapes=[pltpu.VMEM((MLA_HEADS, TQ), F32), pltpu.VMEM((MLA_HEADS, TQ), F32),
                        pltpu.VMEM((hv, TQ), F32)],
        compiler_params=_params(("parallel", "arbitrary"), 40 * 1024 * 1024),
        name="mla_attn",
    )(qn, qp, kn3, kpe3, vt4)


SB_SCALE = SB_HEAD_DIM ** -0.5 * LOG2E
SB_W = SB_HEADS * SB_HEAD_DIM


def _sb_proj_kernel(x_ref, g_ref, wqk_ref, wvt_ref, q_ref, k_ref, vt_ref):
    a = _rmsnorm(x_ref[...], g_ref[...]).astype(BF16)
    qk = _dot(a, wqk_ref[...])
    q_ref[...] = (qk[:, :SB_W] * SB_SCALE).astype(BF16)
    k_ref[...] = qk[:, SB_W:].astype(BF16)
    vt = _dot_nt(wvt_ref[...], a)
    for c in range(vt_ref.shape[0]):
        vt_ref[c] = vt[:, c * TK:(c + 1) * TK].astype(BF16)


def _sb_proj(x2, g, w_qkv):
    t, d = x2.shape
    wqk = w_qkv[:, :2 * SB_W].astype(BF16)
    wvt = w_qkv[:, 2 * SB_W:].T.astype(BF16)
    tm = TM
    row = lambda i: (i, 0)
    return pl.pallas_call(
        _sb_proj_kernel,
        grid=(t // tm,),
        in_specs=[pl.BlockSpec((tm, d), row), _const_spec((1, d)),
                  _const_spec(wqk.shape), _const_spec(wvt.shape)],
        out_specs=[pl.BlockSpec((tm, SB_W), row), pl.BlockSpec((tm, SB_W), row),
                   pl.BlockSpec((tm // TK, SB_W, TK), lambda i: (i, 0, 0))],
        out_shape=[jax.ShapeDtypeStruct((t, SB_W), BF16), jax.ShapeDtypeStruct((t, SB_W), BF16),
                   jax.ShapeDtypeStruct((t // TK, SB_W, TK), BF16)],
        compiler_params=_params(("parallel",), 48 * 1024 * 1024),
        name="sb_proj",
    )(x2, g.reshape(1, d), wqk, wvt)


def _sb_attn_kernel(q_ref, k_ref, vt_ref, o_ref, carry_scr, acc_scr):
    qi = pl.program_id(1)
    heads = SB_HEADS
    hd = SB_HEAD_DIM
    row = lax.broadcasted_iota(jnp.int32, (TK, TK), 0)
    col = lax.broadcasted_iota(jnp.int32, (TK, TK), 1)
    suffix = jnp.where(col > row, 1.0, 0.0).astype(BF16)

    def chunk(kj, diag):
        koff = pl.multiple_of(kj * TK, TK)
        if diag:
            key = lax.broadcasted_iota(jnp.int32, (TK, TQ), 0)
            qry = lax.broadcasted_iota(jnp.int32, (TK, TQ), 1)
            strict = key < qry
        for h in range(heads):
            cols = slice(h * hd, (h + 1) * hd)
            z = _dot_nt(k_ref[pl.ds(koff, TK), cols], q_ref[:, cols])
            sp = jnp.maximum(z, 0.0) + jnp.log2(1.0 + jnp.exp2(-jnp.abs(z)))
            log_beta = z - sp
            if diag:
                sp = jnp.where(strict, sp, 0.0)
            later = _dot(suffix, sp.astype(BF16))
            total = later[0:1, :] + sp[0:1, :]
            if diag:
                arg = log_beta - later
                carry_scr[h:h + 1, :] = total
            else:
                seen = carry_scr[h:h + 1, :]
                arg = log_beta - later - seen
                carry_scr[h:h + 1, :] = seen + total
            a = jnp.exp2(arg)
            if diag:
                a = jnp.where(strict, a, 0.0)
            av = _dot(vt_ref[kj, cols, :], a.astype(BF16))
            if diag:
                acc_scr[cols, :] = av
            else:
                acc_scr[cols, :] = acc_scr[cols, :] + av

    chunk(qi, True)

    def body(i, carry):
        chunk(qi - 1 - i, False)
        return carry

    lax.fori_loop(0, qi, body, 0)

    for h in range(heads):
        cols = slice(h * hd, (h + 1) * hd)
        o_ref[:, cols] = acc_scr[cols, :].T.astype(BF16)


def _sb_attn(q, k, vt, batch, seq):
    t = q.shape[0]
    nq = seq // TQ
    nch = seq // TK
    k3 = k.reshape(batch, seq, SB_W)
    vt4 = vt.reshape(batch, nch, SB_W, TK)
    qrow = lambda b, i: (b * nq + i, 0)
    return pl.pallas_call(
        _sb_attn_kernel,
        grid=(batch, nq),
        in_specs=[pl.BlockSpec((TQ, SB_W), qrow),
                  pl.BlockSpec((None, seq, SB_W), lambda b, i: (b, 0, 0)),
                  pl.BlockSpec((None, nch, SB_W, TK), lambda b, i: (b, 0, 0, 0))],
        out_specs=pl.BlockSpec((TQ, SB_W), qrow),
        out_shape=jax.ShapeDtypeStruct((t, SB_W), BF16),
        scratch_shapes=[pltpu.VMEM((SB_HEADS, TQ), F32), pltpu.VMEM((SB_W, TQ), F32)],
        compiler_params=_params(("parallel", "arbitrary"), 40 * 1024 * 1024),
        name="sb_attn",
    )(q, k3, vt4)


MEM_SCALE = MEM_HEAD_DIM ** -0.5 * LOG2E
MEM_W = MEM_HEADS * MEM_HEAD_DIM


def _cross_kernel(h_ref, o_ref, wo_ref, g_ref, wq_ref, kv_ref, wxo_ref, out_ref):
    h1 = h_ref[...] + _dot(o_ref[...], wo_ref[...])
    n = _rmsnorm(h1, g_ref[...]).astype(BF16)
    q = (_dot(n, wq_ref[...]) * MEM_SCALE).astype(BF16)
    outs = []
    for hh in range(MEM_HEADS):
        cols = slice(hh * MEM_HEAD_DIM, (hh + 1) * MEM_HEAD_DIM)
        vcols = slice(MEM_W + hh * MEM_HEAD_DIM, MEM_W + (hh + 1) * MEM_HEAD_DIM)
        s = _dot_nt(q[:, cols], kv_ref[:, cols])
        p = jnp.exp2(s - jnp.max(s, axis=-1, keepdims=True))
        denom = jnp.sum(p, axis=-1, keepdims=True)
        outs.append((_dot(p.astype(BF16), kv_ref[:, vcols]) / denom).astype(BF16))
    o = jnp.concatenate(outs, axis=1)
    out_ref[...] = h1 + _dot(o, wxo_ref[...])


def _cross(h2, o, w_o, g, w_q, kv, w_xo, seq):
    t, d = h2.shape
    tm = TM
    per_batch = seq // tm
    row = lambda i: (i, 0)
    wo = w_o.astype(BF16)
    wq = w_q.astype(BF16)
    wxo = w_xo.astype(BF16)
    return pl.pallas_call(
        _cross_kernel,
        grid=(t // tm,),
        in_specs=[pl.BlockSpec((tm, d), row), pl.BlockSpec((tm, o.shape[1]), row),
                  _const_spec(wo.shape), _const_spec((1, d)), _const_spec(wq.shape),
                  pl.BlockSpec((None,) + kv.shape[1:], lambda i: (i // per_batch, 0, 0)),
                  _const_spec(wxo.shape)],
        out_specs=pl.BlockSpec((tm, d), row),
        out_shape=jax.ShapeDtypeStruct((t, d), F32),
        compiler_params=_params(("parallel",), 48 * 1024 * 1024),
        name="cross_attn",
    )(h2, o, wo, g.reshape(1, d), wq, kv, wxo)


def _mlp_kernel(h_ref, g_ref, win_ref, wout_ref, gf_ref, out_ref, *, final_norm):
    h = h_ref[...]
    n = _rmsnorm(h, g_ref[...]).astype(BF16)
    acc = h
    d_ff = win_ref.shape[1]
    for c in range(d_ff // FF_CHUNK):
        cols = slice(c * FF_CHUNK, (c + 1) * FF_CHUNK)
        u = jnp.maximum(_dot(n, win_ref[:, cols]), 0.0)
        acc = acc + _dot((u * u).astype(BF16), wout_ref[cols, :])
    if final_norm:
        acc = _rmsnorm(acc, gf_ref[...])
    out_ref[...] = acc


def _mlp(h2, g, w_in, w_out, g_final, final_norm):
    t, d = h2.shape
    tm = TM
    row = lambda i: (i, 0)
    win = w_in.astype(BF16)
    wout = w_out.astype(BF16)
    return pl.pallas_call(
        functools.partial(_mlp_kernel, final_norm=final_norm),
        grid=(t // tm,),
        in_specs=[pl.BlockSpec((tm, d), row), _const_spec((1, d)),
                  _const_spec(win.shape), _const_spec(wout.shape), _const_spec((1, d))],
        out_specs=pl.BlockSpec((tm, d), row),
        out_shape=jax.ShapeDtypeStruct((t, d), F32),
        compiler_params=_params(("parallel",), 56 * 1024 * 1024),
        name="mlp",
    )(h2, g.reshape(1, d), win, wout, g_final.reshape(1, d))


def kernel(x, mem, positions, norm_mix, norm_cross, norm_mem, norm_mlp, norm_final,
           mla_w_dkv, mla_g_q, mla_g_kv, mla_w_uq, mla_w_ukv, mla_w_o,
           sb_w_qkv, sb_w_o, xa_w_q, xa_w_kv, xa_w_o, mlp_w_in, mlp_w_out):
    batch, seq, d = x.shape
    depth = norm_mix.shape[0]
    assert depth == 2 and seq % TM == 0 and TM % TK == 0 and TQ == TK
    h = x.reshape(batch * seq, d)
    cos, sin = _rope_tables(positions)
    mem_kv = _mem_kv(mem, norm_mem, xa_w_kv.astype(BF16))

    for i in range(depth):
        if i % 2 == 0:
            j = i // 2
            qn, qp, kn, kpe, vt = _mla_proj(h, norm_mix[i], mla_w_dkv[j], mla_g_q[j],
                                            mla_g_kv[j], mla_w_uq[j], mla_w_ukv[j], cos, sin)
            o = _mla_attn(qn, qp, kn, kpe, vt, batch, seq)
            w_o = mla_w_o[j]
        else:
            j = i // 2
            q, k, vt = _sb_proj(h, norm_mix[i], sb_w_qkv[j])
            o = _sb_attn(q, k, vt, batch, seq)
            w_o = sb_w_o[j]
        h = _cross(h, o, w_o, norm_cross[i], xa_w_q[i], mem_kv[i], xa_w_o[i], seq)
        h = _mlp(h, norm_mlp[i], mlp_w_in[i], mlp_w_out[i], norm_final,
                 final_norm=(i == depth - 1))
    return h.reshape(batch, seq, d)
```

```python
import functools
import math

import jax
import jax.numpy as jnp
from jax import lax
from jax.experimental import pallas as pl
from jax.experimental.pallas import tpu as pltpu

F32 = jnp.float32
BF16 = jnp.bfloat16

D_MODEL = 1024
EPS = 1e-6
MLA_HEADS = 8
MLA_Q_LORA = 384
MLA_KV_LORA = 256
MLA_NOPE = 128
MLA_ROPE = 64
MLA_V = 128
ROPE_THETA = 10000.0
SB_HEADS = 8
SB_HEAD_DIM = 128
MEM_HEADS = 4
MEM_HEAD_DIM = 128
NEG_BIG = -1e30
LOG2E = math.log2(math.e)

LANES = 128
V7X_VMEM_BYTES = 64 * 1024 * 1024

TQ = 256
TK = 256
TM = 512
FF_CHUNK = 1024


def _vmem_limit(nbytes):
    return int(min(max(nbytes, 32 * 1024 * 1024), V7X_VMEM_BYTES - 8 * 1024 * 1024))


def _params(semantics, vmem_bytes):
    return pltpu.CompilerParams(dimension_semantics=semantics,
                                vmem_limit_bytes=_vmem_limit(vmem_bytes))


def _const_spec(shape):
    nd = len(shape)
    return pl.BlockSpec(shape, lambda *_: (0,) * nd)


def _rmsnorm(x, g):
    return x * lax.rsqrt(jnp.mean(x * x, axis=-1, keepdims=True) + EPS) * g


def _dot(a, b):
    return jnp.dot(a, b, preferred_element_type=F32)


def _dot_nt(a, b):
    return lax.dot_general(a, b, (((1,), (1,)), ((), ())), preferred_element_type=F32)


def _rope_kernel(pos_ref, freq_ref, cos_ref, sin_ref):
    ang = pos_ref[...].astype(F32) * freq_ref[...]
    cos_ref[...] = jnp.cos(ang)
    sin_ref[...] = jnp.sin(ang)


def _rope_tables(positions):
    n_tok = positions.size
    half = MLA_ROPE // 2
    inv_freq = ROPE_THETA ** (-jnp.arange(0, MLA_ROPE, 2, dtype=F32) / MLA_ROPE)
    per_row = LANES // half
    rows = n_tok // per_row
    pos_rep = jnp.repeat(positions.reshape(rows, per_row), half, axis=1)
    freq = jnp.tile(inv_freq, per_row).reshape(1, LANES)
    rb = 512
    cos, sin = pl.pallas_call(
        _rope_kernel,
        grid=(rows // rb,),
        in_specs=[pl.BlockSpec((rb, LANES), lambda i: (i, 0)), _const_spec((1, LANES))],
        out_specs=[pl.BlockSpec((rb, LANES), lambda i: (i, 0))] * 2,
        out_shape=[jax.ShapeDtypeStruct((rows, LANES), F32)] * 2,
        compiler_params=_params(("parallel",), 0),
        name="rope_tables",
    )(pos_rep, freq)
    cos = jnp.tile(cos.reshape(n_tok, half), (1, per_row))
    sin = jnp.tile(sin.reshape(n_tok, half), (1, per_row))
    return cos, sin


def _mem_kv_kernel(mem_ref, g_ref, w_ref, kv_ref):
    n = _rmsnorm(mem_ref[...], g_ref[...]).astype(BF16)
    kv_ref[...] = _dot(n, w_ref[...]).astype(BF16)


def _mem_kv(mem, norm_mem, w_kv):
    depth = norm_mem.shape[0]
    b, m, d = mem.shape
    n_out = w_kv.shape[-1]
    return pl.pallas_call(
        _mem_kv_kernel,
        grid=(depth, b),
        in_specs=[pl.BlockSpec((None, m, d), lambda l, i: (i, 0, 0)),
                  pl.BlockSpec((None, 1, d), lambda l, i: (l, 0, 0)),
                  pl.BlockSpec((None, d, n_out), lambda l, i: (l, 0, 0))],
        out_specs=pl.BlockSpec((None, None, m, n_out), lambda l, i: (l, i, 0, 0)),
        out_shape=jax.ShapeDtypeStruct((depth, b, m, n_out), BF16),
        compiler_params=_params(("parallel", "parallel"), 0),
        name="mem_kv",
    )(mem, norm_mem.reshape(depth, 1, d), w_kv)


MLA_QK_SCALE = (MLA_NOPE + MLA_ROPE) ** -0.5 * LOG2E
LAT_Q_END = MLA_Q_LORA
LAT_KV_END = MLA_Q_LORA + MLA_KV_LORA
LAT_EXT = LAT_KV_END + 2 * MLA_ROPE
Q_NOPE_W = MLA_HEADS * MLA_NOPE
Q_ROPE_W = MLA_HEADS * MLA_ROPE


def _mla_proj_kernel(x_ref, g_ref, wdkv_ref, gq_ref, gkv_ref, wuq_ref, wuk_ref, wvt_ref,
                     cos_ref, sin_ref, qn_ref, qp_ref, kn_ref, kpe_ref, vt_ref):
    a = _rmsnorm(x_ref[...], g_ref[...]).astype(BF16)
    lat = _dot(a, wdkv_ref[...])
    c_q = _rmsnorm(lat[:, :LAT_Q_END], gq_ref[...]).astype(BF16)
    c_kv = _rmsnorm(lat[:, LAT_Q_END:LAT_KV_END], gkv_ref[...]).astype(BF16)
    cos = cos_ref[...]
    sin = sin_ref[...]

    slab = lat[:, LAT_KV_END:LAT_EXT]
    rot = slab * cos + pltpu.roll(slab, MLA_ROPE, 1) * sin
    lane = lax.broadcasted_iota(jnp.int32, rot.shape, 1)
    k_even = jnp.where(lane < MLA_ROPE, rot, 0.0)
    k_odd = pltpu.roll(k_even, MLA_ROPE, 1)
    kpe_ref[...] = jnp.concatenate([k_even, k_odd], axis=1).astype(BF16)

    q = _dot(c_q, wuq_ref[...])
    qn_ref[...] = (q[:, :Q_NOPE_W] * MLA_QK_SCALE).astype(BF16)
    for s in range(Q_ROPE_W // LANES):
        lo = Q_NOPE_W + s * LANES
        r = q[:, lo:lo + LANES]
        r_partner = q[:, lo + Q_ROPE_W:lo + Q_ROPE_W + LANES]
        qp_ref[:, s * LANES:(s + 1) * LANES] = (
            (r * cos + r_partner * sin) * MLA_QK_SCALE).astype(BF16)

    kn_ref[...] = _dot(c_kv, wuk_ref[...]).astype(BF16)
    vt = _dot_nt(wvt_ref[...], c_kv)
    for c in range(vt_ref.shape[0]):
        vt_ref[c] = vt[:, c * TK:(c + 1) * TK].astype(BF16)


def _mla_proj(x2, g, w_dkv, g_q, g_kv, w_uq, w_ukv, cos, sin):
    t, d = x2.shape
    h = MLA_HEADS
    kpe0 = LAT_KV_END
    half = MLA_ROPE // 2
    wdkv = jnp.concatenate(
        [w_dkv, -w_dkv[:, kpe0 + half:kpe0 + MLA_ROPE], w_dkv[:, kpe0:kpe0 + half]],
        axis=1).astype(BF16)
    uq = w_uq.reshape(MLA_Q_LORA, h, MLA_NOPE + MLA_ROPE)
    uq_rope = uq[:, :, MLA_NOPE:]
    uq_partner = jnp.concatenate([-uq_rope[:, :, half:], uq_rope[:, :, :half]], axis=-1)
    wuq = jnp.concatenate(
        [uq[:, :, :MLA_NOPE].reshape(MLA_Q_LORA, Q_NOPE_W),
         uq_rope.reshape(MLA_Q_LORA, Q_ROPE_W),
         uq_partner.reshape(MLA_Q_LORA, Q_ROPE_W)], axis=1).astype(BF16)
    ukv = w_ukv.reshape(MLA_KV_LORA, h, MLA_NOPE + MLA_V)
    wuk = ukv[:, :, :MLA_NOPE].reshape(MLA_KV_LORA, h * MLA_NOPE).astype(BF16)
    wvt = ukv[:, :, MLA_NOPE:].reshape(MLA_KV_LORA, h * MLA_V).T.astype(BF16)

    tm = TM
    n_chunks = t // TK
    row = lambda i: (i, 0)
    out_shapes = [
        jax.ShapeDtypeStruct((t, Q_NOPE_W), BF16),
        jax.ShapeDtypeStruct((t, Q_ROPE_W), BF16),
        jax.ShapeDtypeStruct((t, h * MLA_NOPE), BF16),
        jax.ShapeDtypeStruct((t, 2 * LANES), BF16),
        jax.ShapeDtypeStruct((n_chunks, h * MLA_V, TK), BF16),
    ]
    return pl.pallas_call(
        _mla_proj_kernel,
        grid=(t // tm,),
        in_specs=[pl.BlockSpec((tm, d), row), _const_spec((1, d)),
                  _const_spec(wdkv.shape), _const_spec((1, MLA_Q_LORA)),
                  _const_spec((1, MLA_KV_LORA)), _const_spec(wuq.shape),
                  _const_spec(wuk.shape), _const_spec(wvt.shape),
                  pl.BlockSpec((tm, LANES), row), pl.BlockSpec((tm, LANES), row)],
        out_specs=[pl.BlockSpec((tm, Q_NOPE_W), row), pl.BlockSpec((tm, Q_ROPE_W), row),
                   pl.BlockSpec((tm, h * MLA_NOPE), row), pl.BlockSpec((tm, 2 * LANES), row),
                   pl.BlockSpec((tm // TK, h * MLA_V, TK), lambda i: (i, 0, 0))],
        out_shape=out_shapes,
        compiler_params=_params(("parallel",), 48 * 1024 * 1024),
        name="mla_proj",
    )(x2, g.reshape(1, d), wdkv, g_q.reshape(1, -1), g_kv.reshape(1, -1), wuq, wuk, wvt,
      cos, sin)


def _mla_attn_kernel(qn_ref, qp_ref, kn_ref, kpe_ref, vt_ref, o_ref,
                     m_scr, l_scr, acc_scr, s_scr):
    qi = pl.program_id(1)
    heads = MLA_HEADS
    hd = MLA_NOPE

    def chunk(kj, diag):
        koff = pl.multiple_of(kj * TK, TK)
        kpe = kpe_ref[pl.ds(koff, TK), :]
        if diag:
            key = lax.broadcasted_iota(jnp.int32, (TK, TQ), 0)
            qry = lax.broadcasted_iota(jnp.int32, (TK, TQ), 1)
            causal = key <= qry
        m_new_l, alpha_l = [], []
        for h in range(heads):
            cols = slice(h * hd, (h + 1) * hd)
            pair = slice((h // 2) * LANES, (h // 2 + 1) * LANES)
            par = slice((h % 2) * LANES, (h % 2 + 1) * LANES)
            q_h = jnp.concatenate([qn_ref[:, cols], qp_ref[:, pair]], axis=1)
            k_h = jnp.concatenate([kn_ref[pl.ds(koff, TK), cols], kpe[:, par]], axis=1)
            s = _dot_nt(k_h, q_h)
            if diag:
                s = jnp.where(causal, s, NEG_BIG)
            s_scr[h] = s
            m_cur = jnp.max(s, axis=0, keepdims=True)
            if diag:
                m_new = m_cur
                alpha_l.append(None)
            else:
                m_prev = m_scr[h:h + 1, :]
                m_new = jnp.maximum(m_prev, m_cur)
                alpha_l.append(jnp.exp2(m_prev - m_new))
            m_scr[h:h + 1, :] = m_new
            m_new_l.append(m_new)
        p_bf = []
        for h in range(heads):
            p = jnp.exp2(s_scr[h] - m_new_l[h])
            p_sum = jnp.sum(p, axis=0, keepdims=True)
            if diag:
                l_scr[h:h + 1, :] = p_sum
            else:
                l_scr[h:h + 1, :] = alpha_l[h] * l_scr[h:h + 1, :] + p_sum
            p_bf.append(p.astype(BF16))
        for h in range(heads):
            cols = slice(h * hd, (h + 1) * hd)
            pv = _dot(vt_ref[kj, cols, :], p_bf[h])
            if diag:
                acc_scr[cols, :] = pv
            else:
                acc_scr[cols, :] = alpha_l[h] * acc_scr[cols, :] + pv

    chunk(qi, True)

    def body(kj, carry):
        chunk(kj, False)
        return carry

    lax.fori_loop(0, qi, body, 0)

    for h in range(heads):
        cols = slice(h * hd, (h + 1) * hd)
        o_t = acc_scr[cols, :] / l_scr[h:h + 1, :]
        o_ref[:, cols] = o_t.T.astype(BF16)


def _mla_attn(qn, qp, kn, kpe, vt, batch, seq):
    t = qn.shape[0]
    nq = seq // TQ
    nch = seq // TK
    hv = MLA_HEADS * MLA_V
    kn3 = kn.reshape(batch, seq, -1)
    kpe3 = kpe.reshape(batch, seq, -1)
    vt4 = vt.reshape(batch, nch, hv, TK)
    qrow = lambda b, i: (b * nq + i, 0)
    perb = lambda b, i: (b, 0, 0)
    return pl.pallas_call(
        _mla_attn_kernel,
        grid=(batch, nq),
        in_specs=[pl.BlockSpec((TQ, qn.shape[1]), qrow), pl.BlockSpec((TQ, qp.shape[1]), qrow),
                  pl.BlockSpec((None, seq, kn3.shape[2]), perb),
                  pl.BlockSpec((None, seq, kpe3.shape[2]), perb),
                  pl.BlockSpec((None, nch, hv, TK), lambda b, i: (b, 0, 0, 0))],
        out_specs=pl.BlockSpec((TQ, hv), qrow),
        out_shape=jax.ShapeDtypeStruct((t, hv), BF16),
        scratch_shapes=[pltpu.VMEM((MLA_HEADS, TQ), F32), pltpu.VMEM((MLA_HEADS, TQ), F32),
                        pltpu.VMEM((hv, TQ), F32), pltpu.VMEM((MLA_HEADS, TK, TQ), F32)],
        compiler_params=_params(("parallel", "arbitrary"), 40 * 1024 * 1024),
        name="mla_attn",
    )(qn, qp, kn3, kpe3, vt4)


SB_SCALE = SB_HEAD_DIM ** -0.5 * LOG2E
SB_W = SB_HEADS * SB_HEAD_DIM


def _sb_proj_kernel(x_ref, g_ref, wqk_ref, wvt_ref, q_ref, k_ref, vt_ref):
    a = _rmsnorm(x_ref[...], g_ref[...]).astype(BF16)
    qk = _dot(a, wqk_ref[...])
    q_ref[...] = (qk[:, :SB_W] * SB_SCALE).astype(BF16)
    k_ref[...] = qk[:, SB_W:].astype(BF16)
    vt = _dot_nt(wvt_ref[...], a)
    for c in range(vt_ref.shape[0]):
        vt_ref[c] = vt[:, c * TK:(c + 1) * TK].astype(BF16)


def _sb_proj(x2, g, w_qkv):
    t, d = x2.shape
    wqk = w_qkv[:, :2 * SB_W].astype(BF16)
    wvt = w_qkv[:, 2 * SB_W:].T.astype(BF16)
    tm = TM
    row = lambda i: (i, 0)
    return pl.pallas_call(
        _sb_proj_kernel,
        grid=(t // tm,),
        in_specs=[pl.BlockSpec((tm, d), row), _const_spec((1, d)),
                  _const_spec(wqk.shape), _const_spec(wvt.shape)],
        out_specs=[pl.BlockSpec((tm, SB_W), row), pl.BlockSpec((tm, SB_W), row),
                   pl.BlockSpec((tm // TK, SB_W, TK), lambda i: (i, 0, 0))],
        out_shape=[jax.ShapeDtypeStruct((t, SB_W), BF16), jax.ShapeDtypeStruct((t, SB_W), BF16),
                   jax.ShapeDtypeStruct((t // TK, SB_W, TK), BF16)],
        compiler_params=_params(("parallel",), 48 * 1024 * 1024),
        name="sb_proj",
    )(x2, g.reshape(1, d), wqk, wvt)


def _neg_abs(z):
    bits = lax.bitcast_convert_type(z, jnp.uint32) | jnp.uint32(0x80000000)
    return lax.bitcast_convert_type(bits, F32)


def _sb_attn_kernel(q_ref, k_ref, vt_ref, o_ref, carry_scr, acc_scr, lb_scr):
    qi = pl.program_id(1)
    heads = SB_HEADS
    hd = SB_HEAD_DIM
    row = lax.broadcasted_iota(jnp.int32, (TK, TK), 0)
    col = lax.broadcasted_iota(jnp.int32, (TK, TK), 1)
    suffix = jnp.where((col > row) | (col == 0), 1.0, 0.0).astype(BF16)
    first_row = lax.broadcasted_iota(jnp.int32, (8, TQ), 0) == 0

    def chunk(kj, diag):
        koff = pl.multiple_of(kj * TK, TK)
        if diag:
            key = lax.broadcasted_iota(jnp.int32, (TK, TQ), 0)
            qry = lax.broadcasted_iota(jnp.int32, (TK, TQ), 1)
            strict = key < qry
        sp_bf, sp_key0, seen_l = [], [], []
        for h in range(heads):
            cols = slice(h * hd, (h + 1) * hd)
            z = _dot_nt(k_ref[pl.ds(koff, TK), cols], q_ref[:, cols])
            sp = jnp.maximum(z, 0.0) + jnp.log2(1.0 + jnp.exp2(_neg_abs(z)))
            lb_scr[h] = z - sp
            if diag:
                sp = jnp.where(strict, sp, 0.0)
                seen = jnp.zeros((1, TQ), F32)
            else:
                seen = carry_scr[h:h + 1, :]
            top = sp[0:8, :]
            sp_key0.append(top[0:1, :])
            seen_l.append(seen)
            sp = jnp.concatenate([jnp.where(first_row, seen, top), sp[8:, :]], axis=0)
            sp_bf.append(sp.astype(BF16))
        a_bf = []
        for h in range(heads):
            later = _dot(suffix, sp_bf[h])
            seen = seen_l[h]
            seen_as_summed = seen.astype(BF16).astype(F32)
            carry_scr[h:h + 1, :] = seen + sp_key0[h] + (later[0:1, :] - seen_as_summed)
            a = jnp.exp2(lb_scr[h] - later)
            if diag:
                a = jnp.where(strict, a, 0.0)
            a_bf.append(a.astype(BF16))
        for h in range(heads):
            cols = slice(h * hd, (h + 1) * hd)
            av = _dot(vt_ref[kj, cols, :], a_bf[h])
            if diag:
                acc_scr[cols, :] = av
            else:
                acc_scr[cols, :] = acc_scr[cols, :] + av

    chunk(qi, True)

    def body(i, carry):
        chunk(qi - 1 - i, False)
        return carry

    lax.fori_loop(0, qi, body, 0)

    for h in range(heads):
        cols = slice(h * hd, (h + 1) * hd)
        o_ref[:, cols] = acc_scr[cols, :].T.astype(BF16)


def _sb_attn(q, k, vt, batch, seq):
    t = q.shape[0]
    nq = seq // TQ
    nch = seq // TK
    k3 = k.reshape(batch, seq, SB_W)
    vt4 = vt.reshape(batch, nch, SB_W, TK)
    qrow = lambda b, i: (b * nq + i, 0)
    return pl.pallas_call(
        _sb_attn_kernel,
        grid=(batch, nq),
        in_specs=[pl.BlockSpec((TQ, SB_W), qrow),
                  pl.BlockSpec((None, seq, SB_W), lambda b, i: (b, 0, 0)),
                  pl.BlockSpec((None, nch, SB_W, TK), lambda b, i: (b, 0, 0, 0))],
        out_specs=pl.BlockSpec((TQ, SB_W), qrow),
        out_shape=jax.ShapeDtypeStruct((t, SB_W), BF16),
        scratch_shapes=[pltpu.VMEM((SB_HEADS, TQ), F32), pltpu.VMEM((SB_W, TQ), F32),
                        pltpu.VMEM((SB_HEADS, TK, TQ), F32)],
        compiler_params=_params(("parallel", "arbitrary"), 40 * 1024 * 1024),
        name="sb_attn",
    )(q, k3, vt4)


MEM_SCALE = MEM_HEAD_DIM ** -0.5 * LOG2E
MEM_W = MEM_HEADS * MEM_HEAD_DIM


def _cross_kernel(h_ref, o_ref, wo_ref, g_ref, wq_ref, kv_ref, wxo_ref, out_ref):
    h1 = h_ref[...] + _dot(o_ref[...], wo_ref[...])
    n = _rmsnorm(h1, g_ref[...]).astype(BF16)
    q = (_dot(n, wq_ref[...]) * MEM_SCALE).astype(BF16)
    head_cols = [slice(hh * MEM_HEAD_DIM, (hh + 1) * MEM_HEAD_DIM) for hh in range(MEM_HEADS)]
    scores = [_dot_nt(q[:, c], kv_ref[:, c]) for c in head_cols]
    probs, denoms = [], []
    for s in scores:
        p = jnp.exp2(s - jnp.max(s, axis=-1, keepdims=True))
        denoms.append(jnp.sum(p, axis=-1, keepdims=True))
        probs.append(p.astype(BF16))
    outs = []
    for c, p, denom in zip(head_cols, probs, denoms):
        v_h = kv_ref[:, MEM_W + c.start:MEM_W + c.stop]
        outs.append((_dot(p, v_h) / denom).astype(BF16))
    o = jnp.concatenate(outs, axis=1)
    out_ref[...] = h1 + _dot(o, wxo_ref[...])


def _cross(h2, o, w_o, g, w_q, kv, w_xo, seq):
    t, d = h2.shape
    tm = TM
    per_batch = seq // tm
    row = lambda i: (i, 0)
    wo = w_o.astype(BF16)
    wq = w_q.astype(BF16)
    wxo = w_xo.astype(BF16)
    return pl.pallas_call(
        _cross_kernel,
        grid=(t // tm,),
        in_specs=[pl.BlockSpec((tm, d), row), pl.BlockSpec((tm, o.shape[1]), row),
                  _const_spec(wo.shape), _const_spec((1, d)), _const_spec(wq.shape),
                  pl.BlockSpec((None,) + kv.shape[1:], lambda i: (i // per_batch, 0, 0)),
                  _const_spec(wxo.shape)],
        out_specs=pl.BlockSpec((tm, d), row),
        out_shape=jax.ShapeDtypeStruct((t, d), F32),
        compiler_params=_params(("parallel",), 48 * 1024 * 1024),
        name="cross_attn",
    )(h2, o, wo, g.reshape(1, d), wq, kv, wxo)


def _mlp_kernel(h_ref, g_ref, win_ref, wout_ref, gf_ref, out_ref, *, final_norm):
    h = h_ref[...]
    n = _rmsnorm(h, g_ref[...]).astype(BF16)
    acc = h
    d_ff = win_ref.shape[1]
    for c in range(d_ff // FF_CHUNK):
        cols = slice(c * FF_CHUNK, (c + 1) * FF_CHUNK)
        u = jnp.maximum(_dot(n, win_ref[:, cols]), 0.0)
        acc = acc + _dot((u * u).astype(BF16), wout_ref[cols, :])
    if final_norm:
        acc = _rmsnorm(acc, gf_ref[...])
    out_ref[...] = acc


def _mlp(h2, g, w_in, w_out, g_final, final_norm):
    t, d = h2.shape
    tm = TM
    row = lambda i: (i, 0)
    win = w_in.astype(BF16)
    wout = w_out.astype(BF16)
    return pl.pallas_call(
        functools.partial(_mlp_kernel, final_norm=final_norm),
        grid=(t // tm,),
        in_specs=[pl.BlockSpec((tm, d), row), _const_spec((1, d)),
                  _const_spec(win.shape), _const_spec(wout.shape), _const_spec((1, d))],
        out_specs=pl.BlockSpec((tm, d), row),
        out_shape=jax.ShapeDtypeStruct((t, d), F32),
        compiler_params=_params(("parallel",), 56 * 1024 * 1024),
        name="mlp",
    )(h2, g.reshape(1, d), win, wout, g_final.reshape(1, d))


def kernel(x, mem, positions, norm_mix, norm_cross, norm_mem, norm_mlp, norm_final,
           mla_w_dkv, mla_g_q, mla_g_kv, mla_w_uq, mla_w_ukv, mla_w_o,
           sb_w_qkv, sb_w_o, xa_w_q, xa_w_kv, xa_w_o, mlp_w_in, mlp_w_out):
    batch, seq, d = x.shape
    depth = norm_mix.shape[0]
    assert depth == 2 and seq % TM == 0 and TM % TK == 0 and TQ == TK
    h = x.reshape(batch * seq, d)
    cos, sin = _rope_tables(positions)
    mem_kv = _mem_kv(mem, norm_mem, xa_w_kv.astype(BF16))

    for i in range(depth):
        if i % 2 == 0:
            j = i // 2
            qn, qp, kn, kpe, vt = _mla_proj(h, norm_mix[i], mla_w_dkv[j], mla_g_q[j],
                                            mla_g_kv[j], mla_w_uq[j], mla_w_ukv[j], cos, sin)
            o = _mla_attn(qn, qp, kn, kpe, vt, batch, seq)
            w_o = mla_w_o[j]
        else:
            j = i // 2
            q, k, vt = _sb_proj(h, norm_mix[i], sb_w_qkv[j])
            o = _sb_attn(q, k, vt, batch, seq)
            w_o = sb_w_o[j]
        h = _cross(h, o, w_o, norm_cross[i], xa_w_q[i], mem_kv[i], xa_w_o[i], seq)
        h = _mlp(h, norm_mlp[i], mlp_w_in[i], mlp_w_out[i], norm_final,
                 final_norm=(i == depth - 1))
    return h.reshape(batch, seq, d)
```

```python
import functools
import math

import jax
import jax.numpy as jnp
from jax import lax
from jax.experimental import pallas as pl
from jax.experimental.pallas import tpu as pltpu

F32 = jnp.float32
BF16 = jnp.bfloat16

D_MODEL = 1024
EPS = 1e-6
MLA_HEADS = 8
MLA_Q_LORA = 384
MLA_KV_LORA = 256
MLA_NOPE = 128
MLA_ROPE = 64
MLA_V = 128
ROPE_THETA = 10000.0
SB_HEADS = 8
SB_HEAD_DIM = 128
MEM_HEADS = 4
MEM_HEAD_DIM = 128
NEG_BIG = -1e30
LOG2E = math.log2(math.e)

LANES = 128
V7X_VMEM_BYTES = 64 * 1024 * 1024

TQ = 256
TK = 256
TM = 512
FF_CHUNK = 1024


def _vmem_limit(nbytes):
    return int(min(max(nbytes, 32 * 1024 * 1024), V7X_VMEM_BYTES - 8 * 1024 * 1024))


def _params(semantics, vmem_bytes):
    return pltpu.CompilerParams(dimension_semantics=semantics,
                                vmem_limit_bytes=_vmem_limit(vmem_bytes))


def _const_spec(shape):
    nd = len(shape)
    return pl.BlockSpec(shape, lambda *_: (0,) * nd)


def _rmsnorm(x, g):
    return x * lax.rsqrt(jnp.mean(x * x, axis=-1, keepdims=True) + EPS) * g


def _dot(a, b):
    return jnp.dot(a, b, preferred_element_type=F32)


def _dot_nt(a, b):
    return lax.dot_general(a, b, (((1,), (1,)), ((), ())), preferred_element_type=F32)


def _rope_kernel(pos_ref, freq_ref, cos_ref, sin_ref):
    ang = pos_ref[...].astype(F32) * freq_ref[...]
    cos_ref[...] = jnp.cos(ang)
    sin_ref[...] = jnp.sin(ang)


def _rope_tables(positions):
    n_tok = positions.size
    half = MLA_ROPE // 2
    inv_freq = ROPE_THETA ** (-jnp.arange(0, MLA_ROPE, 2, dtype=F32) / MLA_ROPE)
    per_row = LANES // half
    rows = n_tok // per_row
    pos_rep = jnp.repeat(positions.reshape(rows, per_row), half, axis=1)
    freq = jnp.tile(inv_freq, per_row).reshape(1, LANES)
    rb = 512
    cos, sin = pl.pallas_call(
        _rope_kernel,
        grid=(rows // rb,),
        in_specs=[pl.BlockSpec((rb, LANES), lambda i: (i, 0)), _const_spec((1, LANES))],
        out_specs=[pl.BlockSpec((rb, LANES), lambda i: (i, 0))] * 2,
        out_shape=[jax.ShapeDtypeStruct((rows, LANES), F32)] * 2,
        compiler_params=_params(("parallel",), 0),
        name="rope_tables",
    )(pos_rep, freq)
    cos = jnp.tile(cos.reshape(n_tok, half), (1, per_row))
    sin = jnp.tile(sin.reshape(n_tok, half), (1, per_row))
    return cos, sin


def _mem_kv_kernel(mem_ref, g_ref, w_ref, kv_ref):
    n = _rmsnorm(mem_ref[...], g_ref[...]).astype(BF16)
    kv_ref[...] = _dot(n, w_ref[...]).astype(BF16)


def _mem_kv(mem, norm_mem, w_kv):
    depth = norm_mem.shape[0]
    b, m, d = mem.shape
    n_out = w_kv.shape[-1]
    return pl.pallas_call(
        _mem_kv_kernel,
        grid=(depth, b),
        in_specs=[pl.BlockSpec((None, m, d), lambda l, i: (i, 0, 0)),
                  pl.BlockSpec((None, 1, d), lambda l, i: (l, 0, 0)),
                  pl.BlockSpec((None, d, n_out), lambda l, i: (l, 0, 0))],
        out_specs=pl.BlockSpec((None, None, m, n_out), lambda l, i: (l, i, 0, 0)),
        out_shape=jax.ShapeDtypeStruct((depth, b, m, n_out), BF16),
        compiler_params=_params(("parallel", "parallel"), 0),
        name="mem_kv",
    )(mem, norm_mem.reshape(depth, 1, d), w_kv)


MLA_QK_SCALE = (MLA_NOPE + MLA_ROPE) ** -0.5 * LOG2E
LAT_Q_END = MLA_Q_LORA
LAT_KV_END = MLA_Q_LORA + MLA_KV_LORA
LAT_EXT = LAT_KV_END + 2 * MLA_ROPE
Q_NOPE_W = MLA_HEADS * MLA_NOPE
Q_ROPE_W = MLA_HEADS * MLA_ROPE


def _mla_proj_kernel(x_ref, g_ref, wdkv_ref, gq_ref, gkv_ref, wuq_ref, wuk_ref, wvt_ref,
                     cos_ref, sin_ref, qn_ref, qp_ref, kn_ref, kpe_ref, vt_ref):
    a = _rmsnorm(x_ref[...], g_ref[...]).astype(BF16)
    lat = _dot(a, wdkv_ref[...])
    c_q = _rmsnorm(lat[:, :LAT_Q_END], gq_ref[...]).astype(BF16)
    c_kv = _rmsnorm(lat[:, LAT_Q_END:LAT_KV_END], gkv_ref[...]).astype(BF16)
    cos = cos_ref[...]
    sin = sin_ref[...]

    slab = lat[:, LAT_KV_END:LAT_EXT]
    rot = slab * cos + pltpu.roll(slab, MLA_ROPE, 1) * sin
    lane = lax.broadcasted_iota(jnp.int32, rot.shape, 1)
    k_even = jnp.where(lane < MLA_ROPE, rot, 0.0)
    k_odd = pltpu.roll(k_even, MLA_ROPE, 1)
    kpe_ref[...] = jnp.concatenate([k_even, k_odd], axis=1).astype(BF16)

    q = _dot(c_q, wuq_ref[...])
    qn_ref[...] = (q[:, :Q_NOPE_W] * MLA_QK_SCALE).astype(BF16)
    for s in range(Q_ROPE_W // LANES):
        lo = Q_NOPE_W + s * LANES
        r = q[:, lo:lo + LANES]
        r_partner = q[:, lo + Q_ROPE_W:lo + Q_ROPE_W + LANES]
        qp_ref[:, s * LANES:(s + 1) * LANES] = (
            (r * cos + r_partner * sin) * MLA_QK_SCALE).astype(BF16)

    kn_ref[...] = _dot(c_kv, wuk_ref[...]).astype(BF16)
    vt = _dot_nt(wvt_ref[...], c_kv)
    for c in range(vt_ref.shape[0]):
        vt_ref[c] = vt[:, c * TK:(c + 1) * TK].astype(BF16)


def _mla_proj(x2, g, w_dkv, g_q, g_kv, w_uq, w_ukv, cos, sin):
    t, d = x2.shape
    h = MLA_HEADS
    kpe0 = LAT_KV_END
    half = MLA_ROPE // 2
    wdkv = jnp.concatenate(
        [w_dkv, -w_dkv[:, kpe0 + half:kpe0 + MLA_ROPE], w_dkv[:, kpe0:kpe0 + half]],
        axis=1).astype(BF16)
    uq = w_uq.reshape(MLA_Q_LORA, h, MLA_NOPE + MLA_ROPE)
    uq_rope = uq[:, :, MLA_NOPE:]
    uq_partner = jnp.concatenate([-uq_rope[:, :, half:], uq_rope[:, :, :half]], axis=-1)
    wuq = jnp.concatenate(
        [uq[:, :, :MLA_NOPE].reshape(MLA_Q_LORA, Q_NOPE_W),
         uq_rope.reshape(MLA_Q_LORA, Q_ROPE_W),
         uq_partner.reshape(MLA_Q_LORA, Q_ROPE_W)], axis=1).astype(BF16)
    ukv = w_ukv.reshape(MLA_KV_LORA, h, MLA_NOPE + MLA_V)
    wuk = ukv[:, :, :MLA_NOPE].reshape(MLA_KV_LORA, h * MLA_NOPE).astype(BF16)
    wvt = ukv[:, :, MLA_NOPE:].reshape(MLA_KV_LORA, h * MLA_V).T.astype(BF16)

    tm = TM
    n_chunks = t // TK
    row = lambda i: (i, 0)
    out_shapes = [
        jax.ShapeDtypeStruct((t, Q_NOPE_W), BF16),
        jax.ShapeDtypeStruct((t, Q_ROPE_W), BF16),
        jax.ShapeDtypeStruct((t, h * MLA_NOPE), BF16),
        jax.ShapeDtypeStruct((t, 2 * LANES), BF16),
        jax.ShapeDtypeStruct((n_chunks, h * MLA_V, TK), BF16),
    ]
    return pl.pallas_call(
        _mla_proj_kernel,
        grid=(t // tm,),
        in_specs=[pl.BlockSpec((tm, d), row), _const_spec((1, d)),
                  _const_spec(wdkv.shape), _const_spec((1, MLA_Q_LORA)),
                  _const_spec((1, MLA_KV_LORA)), _const_spec(wuq.shape),
                  _const_spec(wuk.shape), _const_spec(wvt.shape),
                  pl.BlockSpec((tm, LANES), row), pl.BlockSpec((tm, LANES), row)],
        out_specs=[pl.BlockSpec((tm, Q_NOPE_W), row), pl.BlockSpec((tm, Q_ROPE_W), row),
                   pl.BlockSpec((tm, h * MLA_NOPE), row), pl.BlockSpec((tm, 2 * LANES), row),
                   pl.BlockSpec((tm // TK, h * MLA_V, TK), lambda i: (i, 0, 0))],
        out_shape=out_shapes,
        compiler_params=_params(("parallel",), 48 * 1024 * 1024),
        name="mla_proj",
    )(x2, g.reshape(1, d), wdkv, g_q.reshape(1, -1), g_kv.reshape(1, -1), wuq, wuk, wvt,
      cos, sin)


Q_TILES = 2
SUM_ROWS = 16
TQ_STEP = Q_TILES * TQ


def _diag_groups(step, mixer_order):
    groups = []
    for r in range(Q_TILES):
        own = [(r, step * Q_TILES + c, c == r) for c in range(r + 1)]
        groups += own if mixer_order == "any" else own[::-1]
    return groups


def _key_offset(kj):
    off = kj * TK
    return off if isinstance(off, int) else pl.multiple_of(off, TK)


def _mla_attn_kernel(qn_ref, qp_ref, kn_ref, kpe_ref, vt_ref, o_ref, m_scr, l_scr, acc_scr, s_scr):
    step = pl.program_id(1)
    heads, hd = MLA_HEADS, MLA_NOPE
    m_scr[...] = jnp.full(m_scr.shape, NEG_BIG, F32)
    l_scr[...] = jnp.zeros(l_scr.shape, F32)
    acc_scr[...] = jnp.zeros(acc_scr.shape, F32)

    def process(groups):
        units = [(r, kj, masked, h) for r, kj, masked in groups for h in range(heads)]
        key = lax.broadcasted_iota(jnp.int32, (TK, TQ), 0)
        qry = lax.broadcasted_iota(jnp.int32, (TK, TQ), 1)
        stats = []
        for u, (r, kj, masked, h) in enumerate(units):
            koff = _key_offset(kj)
            tile = slice(r * TQ, (r + 1) * TQ)
            cols = slice(h * hd, (h + 1) * hd)
            pair = slice((h // 2) * LANES, (h // 2 + 1) * LANES)
            par = slice((h % 2) * LANES, (h % 2 + 1) * LANES)
            q_h = jnp.concatenate([qn_ref[tile, cols], qp_ref[tile, pair]], axis=1)
            k_h = jnp.concatenate([kn_ref[pl.ds(koff, TK), cols],
                                   kpe_ref[pl.ds(koff, TK), par]], axis=1)
            s = _dot_nt(k_h, q_h)
            if masked:
                s = jnp.where(key <= qry, s, NEG_BIG)
            s_scr[u] = s
            m_prev = m_scr[h:h + 1, tile]
            m_new = jnp.maximum(m_prev, jnp.max(s, axis=0, keepdims=True))
            m_scr[h:h + 1, tile] = m_new
            stats.append((m_new, jnp.exp2(m_prev - m_new)))
        p_bf = [jnp.exp2(s_scr[u] - stats[u][0]).astype(BF16) for u in range(len(units))]
        ones = jnp.ones((SUM_ROWS, TK), BF16)
        for u, (r, kj, masked, h) in enumerate(units):
            tile = slice(r * TQ, (r + 1) * TQ)
            cols = slice(h * hd, (h + 1) * hd)
            alpha = stats[u][1]
            v_ext = jnp.concatenate([vt_ref[kj, cols, :], ones], axis=0)
            pv = _dot(v_ext, p_bf[u])
            acc_scr[cols, tile] = alpha * acc_scr[cols, tile] + pv[:hd, :]
            l_scr[h:h + 1, tile] = alpha * l_scr[h:h + 1, tile] + pv[hd:hd + 1, :]

    process(_diag_groups(step, "any"))

    def body(kj, carry):
        process([(r, kj, False) for r in range(Q_TILES)])
        return carry

    lax.fori_loop(0, step * Q_TILES, body, 0)

    for r in range(Q_TILES):
        tile = slice(r * TQ, (r + 1) * TQ)
        for h in range(heads):
            cols = slice(h * hd, (h + 1) * hd)
            o_t = acc_scr[cols, tile] / l_scr[h:h + 1, tile]
            o_ref[tile, cols] = o_t.T.astype(BF16)


def _mla_attn(qn, qp, kn, kpe, vt, batch, seq):
    t = qn.shape[0]
    steps = seq // TQ_STEP
    nch = seq // TK
    hv = MLA_HEADS * MLA_V
    kn3 = kn.reshape(batch, seq, -1)
    kpe3 = kpe.reshape(batch, seq, -1)
    vt4 = vt.reshape(batch, nch, hv, TK)
    qrow = lambda b, i: (b * steps + i, 0)
    perb = lambda b, i: (b, 0, 0)
    n_units = MLA_HEADS * Q_TILES * (Q_TILES + 1) // 2
    return pl.pallas_call(
        _mla_attn_kernel,
        grid=(batch, steps),
        in_specs=[pl.BlockSpec((TQ_STEP, qn.shape[1]), qrow),
                  pl.BlockSpec((TQ_STEP, qp.shape[1]), qrow),
                  pl.BlockSpec((None, seq, kn3.shape[2]), perb),
                  pl.BlockSpec((None, seq, kpe3.shape[2]), perb),
                  pl.BlockSpec((None, nch, hv, TK), lambda b, i: (b, 0, 0, 0))],
        out_specs=pl.BlockSpec((TQ_STEP, hv), qrow),
        out_shape=jax.ShapeDtypeStruct((t, hv), BF16),
        scratch_shapes=[pltpu.VMEM((MLA_HEADS, TQ_STEP), F32), pltpu.VMEM((MLA_HEADS, TQ_STEP), F32),
                        pltpu.VMEM((hv, TQ_STEP), F32), pltpu.VMEM((n_units, TK, TQ), F32)],
        compiler_params=_params(("parallel", "arbitrary"), 48 * 1024 * 1024),
        name="mla_attn",
    )(qn, qp, kn3, kpe3, vt4)


SB_SCALE = SB_HEAD_DIM ** -0.5 * LOG2E
SB_W = SB_HEADS * SB_HEAD_DIM


def _sb_proj_kernel(x_ref, g_ref, wqk_ref, wvt_ref, q_ref, k_ref, vt_ref):
    a = _rmsnorm(x_ref[...], g_ref[...]).astype(BF16)
    qk = _dot(a, wqk_ref[...])
    q_ref[...] = (qk[:, :SB_W] * SB_SCALE).astype(BF16)
    k_ref[...] = qk[:, SB_W:].astype(BF16)
    vt = _dot_nt(wvt_ref[...], a)
    for c in range(vt_ref.shape[0]):
        vt_ref[c] = vt[:, c * TK:(c + 1) * TK].astype(BF16)


def _sb_proj(x2, g, w_qkv):
    t, d = x2.shape
    wqk = w_qkv[:, :2 * SB_W].astype(BF16)
    wvt = w_qkv[:, 2 * SB_W:].T.astype(BF16)
    tm = TM
    row = lambda i: (i, 0)
    return pl.pallas_call(
        _sb_proj_kernel,
        grid=(t // tm,),
        in_specs=[pl.BlockSpec((tm, d), row), _const_spec((1, d)),
                  _const_spec(wqk.shape), _const_spec(wvt.shape)],
        out_specs=[pl.BlockSpec((tm, SB_W), row), pl.BlockSpec((tm, SB_W), row),
                   pl.BlockSpec((tm // TK, SB_W, TK), lambda i: (i, 0, 0))],
        out_shape=[jax.ShapeDtypeStruct((t, SB_W), BF16), jax.ShapeDtypeStruct((t, SB_W), BF16),
                   jax.ShapeDtypeStruct((t // TK, SB_W, TK), BF16)],
        compiler_params=_params(("parallel",), 48 * 1024 * 1024),
        name="sb_proj",
    )(x2, g.reshape(1, d), wqk, wvt)


def _sb_attn_kernel(q_ref, k_ref, vt_ref, o_ref, carry_scr, acc_scr, lb_scr, top_scr):
    step = pl.program_id(1)
    heads, hd = SB_HEADS, SB_HEAD_DIM
    row = lax.broadcasted_iota(jnp.int32, (TK, TK), 0)
    col = lax.broadcasted_iota(jnp.int32, (TK, TK), 1)
    suffix = jnp.where((col > row) | (col == 0), 1.0, 0.0).astype(BF16)
    first_row = lax.broadcasted_iota(jnp.int32, (SUM_ROWS, TQ), 0) == 0
    carry_scr[...] = jnp.zeros(carry_scr.shape, F32)
    acc_scr[...] = jnp.zeros(acc_scr.shape, F32)

    def process(groups):
        units = [(r, kj, masked, h) for r, kj, masked in groups for h in range(heads)]
        key = lax.broadcasted_iota(jnp.int32, (TK, TQ), 0)
        qry = lax.broadcasted_iota(jnp.int32, (TK, TQ), 1)
        sp_rest = []
        for u, (r, kj, masked, h) in enumerate(units):
            koff = _key_offset(kj)
            tile = slice(r * TQ, (r + 1) * TQ)
            cols = slice(h * hd, (h + 1) * hd)
            z = _dot_nt(k_ref[pl.ds(koff, TK), cols], q_ref[tile, cols])
            sp = jnp.maximum(z, 0.0) + jnp.log2(1.0 + jnp.exp2(-jnp.abs(z)))
            lb_scr[u] = z - sp
            if masked:
                sp = jnp.where(key < qry, sp, 0.0)
            top_scr[u] = sp[:SUM_ROWS, :]
            sp_rest.append(sp[SUM_ROWS:, :].astype(BF16))
        a_bf = []
        for u, (r, kj, masked, h) in enumerate(units):
            tile = slice(r * TQ, (r + 1) * TQ)
            seen = carry_scr[h:h + 1, tile]
            top = top_scr[u]
            rhs = jnp.concatenate(
                [jnp.where(first_row, seen, top).astype(BF16), sp_rest[u]], axis=0)
            later = _dot(suffix, rhs)
            seen_as_summed = seen.astype(BF16).astype(F32)
            carry_scr[h:h + 1, tile] = seen + top[0:1, :] + (later[0:1, :] - seen_as_summed)
            a = jnp.exp2(lb_scr[u] - later)
            if masked:
                a = jnp.where(key < qry, a, 0.0)
            a_bf.append(a.astype(BF16))
        for u, (r, kj, masked, h) in enumerate(units):
            tile = slice(r * TQ, (r + 1) * TQ)
            cols = slice(h * hd, (h + 1) * hd)
            acc_scr[cols, tile] = acc_scr[cols, tile] + _dot(vt_ref[kj, cols, :], a_bf[u])

    process(_diag_groups(step, "right_to_left"))

    def body(i, carry):
        kj = step * Q_TILES - 1 - i
        process([(r, kj, False) for r in range(Q_TILES)])
        return carry

    lax.fori_loop(0, step * Q_TILES, body, 0)

    for r in range(Q_TILES):
        tile = slice(r * TQ, (r + 1) * TQ)
        for h in range(heads):
            cols = slice(h * hd, (h + 1) * hd)
            o_ref[tile, cols] = acc_scr[cols, tile].T.astype(BF16)


def _sb_attn(q, k, vt, batch, seq):
    t = q.shape[0]
    steps = seq // TQ_STEP
    nch = seq // TK
    k3 = k.reshape(batch, seq, SB_W)
    vt4 = vt.reshape(batch, nch, SB_W, TK)
    qrow = lambda b, i: (b * steps + i, 0)
    n_units = SB_HEADS * Q_TILES * (Q_TILES + 1) // 2
    return pl.pallas_call(
        _sb_attn_kernel,
        grid=(batch, steps),
        in_specs=[pl.BlockSpec((TQ_STEP, SB_W), qrow),
                  pl.BlockSpec((None, seq, SB_W), lambda b, i: (b, 0, 0)),
                  pl.BlockSpec((None, nch, SB_W, TK), lambda b, i: (b, 0, 0, 0))],
        out_specs=pl.BlockSpec((TQ_STEP, SB_W), qrow),
        out_shape=jax.ShapeDtypeStruct((t, SB_W), BF16),
        scratch_shapes=[pltpu.VMEM((SB_HEADS, TQ_STEP), F32), pltpu.VMEM((SB_W, TQ_STEP), F32),
                        pltpu.VMEM((n_units, TK, TQ), F32),
                        pltpu.VMEM((n_units, SUM_ROWS, TQ), F32)],
        compiler_params=_params(("parallel", "arbitrary"), 48 * 1024 * 1024),
        name="sb_attn",
    )(q, k3, vt4)


MEM_SCALE = MEM_HEAD_DIM ** -0.5 * LOG2E
MEM_W = MEM_HEADS * MEM_HEAD_DIM


def _cross_kernel(h_ref, o_ref, wo_ref, g_ref, wq_ref, kv_ref, wxo_ref, out_ref):
    h1 = h_ref[...] + _dot(o_ref[...], wo_ref[...])
    n = _rmsnorm(h1, g_ref[...]).astype(BF16)
    q = (_dot(n, wq_ref[...]) * MEM_SCALE).astype(BF16)
    head_cols = [slice(hh * MEM_HEAD_DIM, (hh + 1) * MEM_HEAD_DIM) for hh in range(MEM_HEADS)]
    scores = [_dot_nt(q[:, c], kv_ref[:, c]) for c in head_cols]
    probs, denoms = [], []
    for s in scores:
        p = jnp.exp2(s - jnp.max(s, axis=-1, keepdims=True))
        denoms.append(jnp.sum(p, axis=-1, keepdims=True))
        probs.append(p.astype(BF16))
    outs = []
    for c, p, denom in zip(head_cols, probs, denoms):
        v_h = kv_ref[:, MEM_W + c.start:MEM_W + c.stop]
        outs.append((_dot(p, v_h) / denom).astype(BF16))
    o = jnp.concatenate(outs, axis=1)
    out_ref[...] = h1 + _dot(o, wxo_ref[...])


def _cross(h2, o, w_o, g, w_q, kv, w_xo, seq):
    t, d = h2.shape
    tm = TM
    per_batch = seq // tm
    row = lambda i: (i, 0)
    wo = w_o.astype(BF16)
    wq = w_q.astype(BF16)
    wxo = w_xo.astype(BF16)
    return pl.pallas_call(
        _cross_kernel,
        grid=(t // tm,),
        in_specs=[pl.BlockSpec((tm, d), row), pl.BlockSpec((tm, o.shape[1]), row),
                  _const_spec(wo.shape), _const_spec((1, d)), _const_spec(wq.shape),
                  pl.BlockSpec((None,) + kv.shape[1:], lambda i: (i // per_batch, 0, 0)),
                  _const_spec(wxo.shape)],
        out_specs=pl.BlockSpec((tm, d), row),
        out_shape=jax.ShapeDtypeStruct((t, d), F32),
        compiler_params=_params(("parallel",), 48 * 1024 * 1024),
        name="cross_attn",
    )(h2, o, wo, g.reshape(1, d), wq, kv, wxo)


def _mlp_kernel(h_ref, g_ref, win_ref, wout_ref, gf_ref, out_ref, *, final_norm):
    h = h_ref[...]
    n = _rmsnorm(h, g_ref[...]).astype(BF16)
    acc = h
    d_ff = win_ref.shape[1]
    for c in range(d_ff // FF_CHUNK):
        cols = slice(c * FF_CHUNK, (c + 1) * FF_CHUNK)
        u = jnp.maximum(_dot(n, win_ref[:, cols]), 0.0)
        acc = acc + _dot((u * u).astype(BF16), wout_ref[cols, :])
    if final_norm:
        acc = _rmsnorm(acc, gf_ref[...])
    out_ref[...] = acc


def _mlp(h2, g, w_in, w_out, g_final, final_norm):
    t, d = h2.shape
    tm = TM
    row = lambda i: (i, 0)
    win = w_in.astype(BF16)
    wout = w_out.astype(BF16)
    return pl.pallas_call(
        functools.partial(_mlp_kernel, final_norm=final_norm),
        grid=(t // tm,),
        in_specs=[pl.BlockSpec((tm, d), row), _const_spec((1, d)),
                  _const_spec(win.shape), _const_spec(wout.shape), _const_spec((1, d))],
        out_specs=pl.BlockSpec((tm, d), row),
        out_shape=jax.ShapeDtypeStruct((t, d), F32),
        compiler_params=_params(("parallel",), 56 * 1024 * 1024),
        name="mlp",
    )(h2, g.reshape(1, d), win, wout, g_final.reshape(1, d))


def kernel(x, mem, positions, norm_mix, norm_cross, norm_mem, norm_mlp, norm_final,
           mla_w_dkv, mla_g_q, mla_g_kv, mla_w_uq, mla_w_ukv, mla_w_o,
           sb_w_qkv, sb_w_o, xa_w_q, xa_w_kv, xa_w_o, mlp_w_in, mlp_w_out):
    batch, seq, d = x.shape
    depth = norm_mix.shape[0]
    assert depth == 2 and seq % TM == 0 and TM % TK == 0 and TQ == TK and seq % TQ_STEP == 0
    h = x.reshape(batch * seq, d)
    cos, sin = _rope_tables(positions)
    mem_kv = _mem_kv(mem, norm_mem, xa_w_kv.astype(BF16))

    for i in range(depth):
        if i % 2 == 0:
            j = i // 2
            qn, qp, kn, kpe, vt = _mla_proj(h, norm_mix[i], mla_w_dkv[j], mla_g_q[j],
                                            mla_g_kv[j], mla_w_uq[j], mla_w_ukv[j], cos, sin)
            o = _mla_attn(qn, qp, kn, kpe, vt, batch, seq)
            w_o = mla_w_o[j]
        else:
            j = i // 2
            q, k, vt = _sb_proj(h, norm_mix[i], sb_w_qkv[j])
            o = _sb_attn(q, k, vt, batch, seq)
            w_o = sb_w_o[j]
        h = _cross(h, o, w_o, norm_cross[i], xa_w_q[i], mem_kv[i], xa_w_o[i], seq)
        h = _mlp(h, norm_mlp[i], mlp_w_in[i], mlp_w_out[i], norm_final,
                 final_norm=(i == depth - 1))
    return h.reshape(batch, seq, d)
```

```python
import functools
import math

import jax
import jax.numpy as jnp
from jax import lax
from jax.experimental import pallas as pl
from jax.experimental.pallas import tpu as pltpu

F32 = jnp.float32
BF16 = jnp.bfloat16

D_MODEL = 1024
EPS = 1e-6
MLA_HEADS = 8
MLA_Q_LORA = 384
MLA_KV_LORA = 256
MLA_NOPE = 128
MLA_ROPE = 64
MLA_V = 128
ROPE_THETA = 10000.0
SB_HEADS = 8
SB_HEAD_DIM = 128
MEM_HEADS = 4
MEM_HEAD_DIM = 128
NEG_BIG = -1e30
LOG2E = math.log2(math.e)

LANES = 128
V7X_VMEM_BYTES = 64 * 1024 * 1024

TQ = 256
TK = 256
TM = 512
FF_CHUNK = 1024


def _vmem_limit(nbytes):
    return int(min(max(nbytes, 32 * 1024 * 1024), V7X_VMEM_BYTES - 8 * 1024 * 1024))


def _params(semantics, vmem_bytes):
    return pltpu.CompilerParams(dimension_semantics=semantics,
                                vmem_limit_bytes=_vmem_limit(vmem_bytes))


def _const_spec(shape):
    nd = len(shape)
    return pl.BlockSpec(shape, lambda *_: (0,) * nd)


def _layer_spec(stacked, layer):
    rest = stacked.shape[1:]
    return pl.BlockSpec((None,) + rest, lambda *_: (layer,) + (0,) * len(rest))


def _rows(param):
    return param.reshape(param.shape[0], 1, param.shape[1])


def _rmsnorm(x, g):
    return x * lax.rsqrt(jnp.mean(x * x, axis=-1, keepdims=True) + EPS) * g


def _dot(a, b):
    return jnp.dot(a, b, preferred_element_type=F32)


def _dot_nt(a, b):
    return lax.dot_general(a, b, (((1,), (1,)), ((), ())), preferred_element_type=F32)


def _rope_kernel(pos_ref, freq_ref, cos_ref, sin_ref):
    ang = pos_ref[...].astype(F32) * freq_ref[...]
    cos_ref[...] = jnp.cos(ang)
    sin_ref[...] = jnp.sin(ang)


def _rope_tables(positions):
    n_tok = positions.size
    half = MLA_ROPE // 2
    inv_freq = ROPE_THETA ** (-jnp.arange(0, MLA_ROPE, 2, dtype=F32) / MLA_ROPE)
    per_row = LANES // half
    rows = n_tok // per_row
    pos_rep = jnp.repeat(positions.reshape(rows, per_row), half, axis=1)
    freq = jnp.tile(inv_freq, per_row).reshape(1, LANES)
    rb = 512
    cos, sin = pl.pallas_call(
        _rope_kernel,
        grid=(rows // rb,),
        in_specs=[pl.BlockSpec((rb, LANES), lambda i: (i, 0)), _const_spec((1, LANES))],
        out_specs=[pl.BlockSpec((rb, LANES), lambda i: (i, 0))] * 2,
        out_shape=[jax.ShapeDtypeStruct((rows, LANES), F32)] * 2,
        compiler_params=_params(("parallel",), 0),
        name="rope_tables",
    )(pos_rep, freq)
    cos = jnp.tile(cos.reshape(n_tok, half), (1, per_row))
    sin = jnp.tile(sin.reshape(n_tok, half), (1, per_row))
    return cos, sin


def _mem_kv_kernel(mem_ref, g_ref, w_ref, kv_ref):
    n = _rmsnorm(mem_ref[...], g_ref[...]).astype(BF16)
    kv_ref[...] = _dot(n, w_ref[...]).astype(BF16)


def _mem_kv(mem, norm_mem, w_kv):
    depth = norm_mem.shape[0]
    b, m, d = mem.shape
    n_out = w_kv.shape[-1]
    return pl.pallas_call(
        _mem_kv_kernel,
        grid=(depth, b),
        in_specs=[pl.BlockSpec((None, m, d), lambda l, i: (i, 0, 0)),
                  pl.BlockSpec((None, 1, d), lambda l, i: (l, 0, 0)),
                  pl.BlockSpec((None, d, n_out), lambda l, i: (l, 0, 0))],
        out_specs=pl.BlockSpec((None, None, m, n_out), lambda l, i: (l, i, 0, 0)),
        out_shape=jax.ShapeDtypeStruct((depth, b, m, n_out), BF16),
        compiler_params=_params(("parallel", "parallel"), 0),
        name="mem_kv",
    )(mem, norm_mem.reshape(depth, 1, d), w_kv)


MLA_QK_SCALE = (MLA_NOPE + MLA_ROPE) ** -0.5 * LOG2E
LAT_Q_END = MLA_Q_LORA
LAT_KV_END = MLA_Q_LORA + MLA_KV_LORA
LAT_EXT = LAT_KV_END + 2 * MLA_ROPE
Q_NOPE_W = MLA_HEADS * MLA_NOPE
Q_ROPE_W = MLA_HEADS * MLA_ROPE


def _mla_proj_kernel(x_ref, g_ref, wdkv_ref, gq_ref, gkv_ref, wuq_ref, wuk_ref, wvt_ref,
                     cos_ref, sin_ref, qn_ref, qp_ref, kn_ref, kpe_ref, vt_ref):
    a = _rmsnorm(x_ref[...], g_ref[...]).astype(BF16)
    lat = _dot(a, wdkv_ref[...])
    c_q = _rmsnorm(lat[:, :LAT_Q_END], gq_ref[...]).astype(BF16)
    c_kv = _rmsnorm(lat[:, LAT_Q_END:LAT_KV_END], gkv_ref[...]).astype(BF16)
    cos = cos_ref[...]
    sin = sin_ref[...]

    slab = lat[:, LAT_KV_END:LAT_EXT]
    rot = slab * cos + pltpu.roll(slab, MLA_ROPE, 1) * sin
    lane = lax.broadcasted_iota(jnp.int32, rot.shape, 1)
    k_even = jnp.where(lane < MLA_ROPE, rot, 0.0)
    k_odd = pltpu.roll(k_even, MLA_ROPE, 1)
    kpe_ref[...] = jnp.concatenate([k_even, k_odd], axis=1).astype(BF16)

    q = _dot(c_q, wuq_ref[...])
    qn_ref[...] = (q[:, :Q_NOPE_W] * MLA_QK_SCALE).astype(BF16)
    for s in range(Q_ROPE_W // LANES):
        lo = Q_NOPE_W + s * LANES
        r = q[:, lo:lo + LANES]
        r_partner = q[:, lo + Q_ROPE_W:lo + Q_ROPE_W + LANES]
        qp_ref[:, s * LANES:(s + 1) * LANES] = (
            (r * cos + r_partner * sin) * MLA_QK_SCALE).astype(BF16)

    kn_ref[...] = _dot(c_kv, wuk_ref[...]).astype(BF16)
    vt = _dot_nt(wvt_ref[...], c_kv)
    for c in range(vt_ref.shape[0]):
        vt_ref[c] = vt[:, c * TK:(c + 1) * TK].astype(BF16)


def _mla_proj(x2, g, w_dkv, g_q, g_kv, w_uq, w_ukv, cos, sin):
    t, d = x2.shape
    h = MLA_HEADS
    kpe0 = LAT_KV_END
    half = MLA_ROPE // 2
    wdkv = jnp.concatenate(
        [w_dkv, -w_dkv[:, kpe0 + half:kpe0 + MLA_ROPE], w_dkv[:, kpe0:kpe0 + half]],
        axis=1).astype(BF16)
    uq = w_uq.reshape(MLA_Q_LORA, h, MLA_NOPE + MLA_ROPE)
    uq_rope = uq[:, :, MLA_NOPE:]
    uq_partner = jnp.concatenate([-uq_rope[:, :, half:], uq_rope[:, :, :half]], axis=-1)
    wuq = jnp.concatenate(
        [uq[:, :, :MLA_NOPE].reshape(MLA_Q_LORA, Q_NOPE_W),
         uq_rope.reshape(MLA_Q_LORA, Q_ROPE_W),
         uq_partner.reshape(MLA_Q_LORA, Q_ROPE_W)], axis=1).astype(BF16)
    ukv = w_ukv.reshape(MLA_KV_LORA, h, MLA_NOPE + MLA_V)
    wuk = ukv[:, :, :MLA_NOPE].reshape(MLA_KV_LORA, h * MLA_NOPE).astype(BF16)
    wvt = ukv[:, :, MLA_NOPE:].reshape(MLA_KV_LORA, h * MLA_V).T.astype(BF16)

    tm = TM
    n_chunks = t // TK
    row = lambda i: (i, 0)
    out_shapes = [
        jax.ShapeDtypeStruct((t, Q_NOPE_W), BF16),
        jax.ShapeDtypeStruct((t, Q_ROPE_W), BF16),
        jax.ShapeDtypeStruct((t, h * MLA_NOPE), BF16),
        jax.ShapeDtypeStruct((t, 2 * LANES), BF16),
        jax.ShapeDtypeStruct((n_chunks, h * MLA_V, TK), BF16),
    ]
    return pl.pallas_call(
        _mla_proj_kernel,
        grid=(t // tm,),
        in_specs=[pl.BlockSpec((tm, d), row), _const_spec((1, d)),
                  _const_spec(wdkv.shape), _const_spec((1, MLA_Q_LORA)),
                  _const_spec((1, MLA_KV_LORA)), _const_spec(wuq.shape),
                  _const_spec(wuk.shape), _const_spec(wvt.shape),
                  pl.BlockSpec((tm, LANES), row), pl.BlockSpec((tm, LANES), row)],
        out_specs=[pl.BlockSpec((tm, Q_NOPE_W), row), pl.BlockSpec((tm, Q_ROPE_W), row),
                   pl.BlockSpec((tm, h * MLA_NOPE), row), pl.BlockSpec((tm, 2 * LANES), row),
                   pl.BlockSpec((tm // TK, h * MLA_V, TK), lambda i: (i, 0, 0))],
        out_shape=out_shapes,
        compiler_params=_params(("parallel",), 48 * 1024 * 1024),
        name="mla_proj",
    )(x2, g.reshape(1, d), wdkv, g_q.reshape(1, -1), g_kv.reshape(1, -1), wuq, wuk, wvt,
      cos, sin)


Q_TILES = 2
SUM_ROWS = 16
TQ_STEP = Q_TILES * TQ


def _diag_groups(step, mixer_order):
    groups = []
    for r in range(Q_TILES):
        own = [(r, step * Q_TILES + c, c == r) for c in range(r + 1)]
        groups += own if mixer_order == "any" else own[::-1]
    return groups


def _key_offset(kj):
    off = kj * TK
    return off if isinstance(off, int) else pl.multiple_of(off, TK)


def _mla_attn_kernel(qn_ref, qp_ref, kn_ref, kpe_ref, vt_ref, o_ref, m_scr, l_scr, acc_scr, s_scr):
    step = pl.program_id(1)
    heads, hd = MLA_HEADS, MLA_NOPE
    m_scr[...] = jnp.full(m_scr.shape, NEG_BIG, F32)
    l_scr[...] = jnp.zeros(l_scr.shape, F32)
    acc_scr[...] = jnp.zeros(acc_scr.shape, F32)

    def process(groups):
        units = [(r, kj, masked, h) for r, kj, masked in groups for h in range(heads)]
        key = lax.broadcasted_iota(jnp.int32, (TK, TQ), 0)
        qry = lax.broadcasted_iota(jnp.int32, (TK, TQ), 1)
        stats = []
        for u, (r, kj, masked, h) in enumerate(units):
            koff = _key_offset(kj)
            tile = slice(r * TQ, (r + 1) * TQ)
            cols = slice(h * hd, (h + 1) * hd)
            pair = slice((h // 2) * LANES, (h // 2 + 1) * LANES)
            par = slice((h % 2) * LANES, (h % 2 + 1) * LANES)
            q_h = jnp.concatenate([qn_ref[tile, cols], qp_ref[tile, pair]], axis=1)
            k_h = jnp.concatenate([kn_ref[pl.ds(koff, TK), cols],
                                   kpe_ref[pl.ds(koff, TK), par]], axis=1)
            s = _dot_nt(k_h, q_h)
            if masked:
                s = jnp.where(key <= qry, s, NEG_BIG)
            s_scr[u] = s
            m_prev = m_scr[h:h + 1, tile]
            m_new = jnp.maximum(m_prev, jnp.max(s, axis=0, keepdims=True))
            m_scr[h:h + 1, tile] = m_new
            stats.append((m_new, jnp.exp2(m_prev - m_new)))
        p_bf = [jnp.exp2(s_scr[u] - stats[u][0]).astype(BF16) for u in range(len(units))]
        ones = jnp.ones((SUM_ROWS, TK), BF16)
        for u, (r, kj, masked, h) in enumerate(units):
            tile = slice(r * TQ, (r + 1) * TQ)
            cols = slice(h * hd, (h + 1) * hd)
            alpha = stats[u][1]
            v_ext = jnp.concatenate([vt_ref[kj, cols, :], ones], axis=0)
            pv = _dot(v_ext, p_bf[u])
            acc_scr[cols, tile] = alpha * acc_scr[cols, tile] + pv[:hd, :]
            l_scr[h:h + 1, tile] = alpha * l_scr[h:h + 1, tile] + pv[hd:hd + 1, :]

    process(_diag_groups(step, "any"))

    def body(kj, carry):
        process([(r, kj, False) for r in range(Q_TILES)])
        return carry

    lax.fori_loop(0, step * Q_TILES, body, 0)

    for r in range(Q_TILES):
        tile = slice(r * TQ, (r + 1) * TQ)
        for h in range(heads):
            cols = slice(h * hd, (h + 1) * hd)
            o_t = acc_scr[cols, tile] / l_scr[h:h + 1, tile]
            o_ref[tile, cols] = o_t.T.astype(BF16)


def _mla_attn(qn, qp, kn, kpe, vt, batch, seq):
    t = qn.shape[0]
    steps = seq // TQ_STEP
    nch = seq // TK
    hv = MLA_HEADS * MLA_V
    kn3 = kn.reshape(batch, seq, -1)
    kpe3 = kpe.reshape(batch, seq, -1)
    vt4 = vt.reshape(batch, nch, hv, TK)
    qrow = lambda b, i: (b * steps + i, 0)
    perb = lambda b, i: (b, 0, 0)
    n_units = MLA_HEADS * Q_TILES * (Q_TILES + 1) // 2
    return pl.pallas_call(
        _mla_attn_kernel,
        grid=(batch, steps),
        in_specs=[pl.BlockSpec((TQ_STEP, qn.shape[1]), qrow),
                  pl.BlockSpec((TQ_STEP, qp.shape[1]), qrow),
                  pl.BlockSpec((None, seq, kn3.shape[2]), perb),
                  pl.BlockSpec((None, seq, kpe3.shape[2]), perb),
                  pl.BlockSpec((None, nch, hv, TK), lambda b, i: (b, 0, 0, 0))],
        out_specs=pl.BlockSpec((TQ_STEP, hv), qrow),
        out_shape=jax.ShapeDtypeStruct((t, hv), BF16),
        scratch_shapes=[pltpu.VMEM((MLA_HEADS, TQ_STEP), F32), pltpu.VMEM((MLA_HEADS, TQ_STEP), F32),
                        pltpu.VMEM((hv, TQ_STEP), F32), pltpu.VMEM((n_units, TK, TQ), F32)],
        compiler_params=_params(("parallel", "arbitrary"), 48 * 1024 * 1024),
        name="mla_attn",
    )(qn, qp, kn3, kpe3, vt4)


SB_SCALE = SB_HEAD_DIM ** -0.5 * LOG2E
SB_W = SB_HEADS * SB_HEAD_DIM


def _sb_proj_kernel(x_ref, g_ref, wqk_ref, wvt_ref, q_ref, k_ref, vt_ref):
    a = _rmsnorm(x_ref[...], g_ref[...]).astype(BF16)
    qk = _dot(a, wqk_ref[...])
    q_ref[...] = (qk[:, :SB_W] * SB_SCALE).astype(BF16)
    k_ref[...] = qk[:, SB_W:].astype(BF16)
    vt = _dot_nt(wvt_ref[...], a)
    for c in range(vt_ref.shape[0]):
        vt_ref[c] = vt[:, c * TK:(c + 1) * TK].astype(BF16)


def _sb_proj(x2, g, w_qkv):
    t, d = x2.shape
    wqk = w_qkv[:, :2 * SB_W].astype(BF16)
    wvt = w_qkv[:, 2 * SB_W:].T.astype(BF16)
    tm = TM
    row = lambda i: (i, 0)
    return pl.pallas_call(
        _sb_proj_kernel,
        grid=(t // tm,),
        in_specs=[pl.BlockSpec((tm, d), row), _const_spec((1, d)),
                  _const_spec(wqk.shape), _const_spec(wvt.shape)],
        out_specs=[pl.BlockSpec((tm, SB_W), row), pl.BlockSpec((tm, SB_W), row),
                   pl.BlockSpec((tm // TK, SB_W, TK), lambda i: (i, 0, 0))],
        out_shape=[jax.ShapeDtypeStruct((t, SB_W), BF16), jax.ShapeDtypeStruct((t, SB_W), BF16),
                   jax.ShapeDtypeStruct((t // TK, SB_W, TK), BF16)],
        compiler_params=_params(("parallel",), 48 * 1024 * 1024),
        name="sb_proj",
    )(x2, g.reshape(1, d), wqk, wvt)


SOFTPLUS_LINEAR_FROM = 64.0


def _sb_attn_kernel(q_ref, k_ref, vt_ref, o_ref, carry_scr, acc_scr, lb_scr, top_scr, sp_scr, a_scr):
    step = pl.program_id(1)
    heads, hd = SB_HEADS, SB_HEAD_DIM
    row = lax.broadcasted_iota(jnp.int32, (TK, TK), 0)
    col = lax.broadcasted_iota(jnp.int32, (TK, TK), 1)
    suffix = jnp.where((col > row) | (col == 0), 1.0, 0.0).astype(BF16)
    first_row = lax.broadcasted_iota(jnp.int32, (SUM_ROWS, TQ), 0) == 0
    carry_scr[...] = jnp.zeros(carry_scr.shape, F32)
    acc_scr[...] = jnp.zeros(acc_scr.shape, F32)

    def process(groups):
        units = [(r, kj, masked, h) for r, kj, masked in groups for h in range(heads)]
        key = lax.broadcasted_iota(jnp.int32, (TK, TQ), 0)
        qry = lax.broadcasted_iota(jnp.int32, (TK, TQ), 1)
        for u, (r, kj, masked, h) in enumerate(units):
            koff = _key_offset(kj)
            tile = slice(r * TQ, (r + 1) * TQ)
            cols = slice(h * hd, (h + 1) * hd)
            z = _dot_nt(k_ref[pl.ds(koff, TK), cols], q_ref[tile, cols])
            sp = jnp.where(z > SOFTPLUS_LINEAR_FROM, z, jnp.log2(1.0 + jnp.exp2(z)))
            lb_scr[u] = z - sp
            if masked:
                sp = jnp.where(key < qry, sp, 0.0)
            top_scr[u] = sp[:SUM_ROWS, :]
            sp_scr[u] = sp[SUM_ROWS:, :].astype(BF16)
        for u, (r, kj, masked, h) in enumerate(units):
            tile = slice(r * TQ, (r + 1) * TQ)
            seen = carry_scr[h:h + 1, tile]
            top = top_scr[u]
            rhs = jnp.concatenate(
                [jnp.where(first_row, seen, top).astype(BF16), sp_scr[u]], axis=0)
            later = _dot(suffix, rhs)
            seen_as_summed = seen.astype(BF16).astype(F32)
            carry_scr[h:h + 1, tile] = seen + top[0:1, :] + (later[0:1, :] - seen_as_summed)
            a = jnp.exp2(lb_scr[u] - later)
            if masked:
                a = jnp.where(key < qry, a, 0.0)
            a_scr[u] = a.astype(BF16)
        for u, (r, kj, masked, h) in enumerate(units):
            tile = slice(r * TQ, (r + 1) * TQ)
            cols = slice(h * hd, (h + 1) * hd)
            acc_scr[cols, tile] = acc_scr[cols, tile] + _dot(vt_ref[kj, cols, :], a_scr[u])

    process(_diag_groups(step, "right_to_left"))

    def body(i, carry):
        kj = step * Q_TILES - 1 - i
        process([(r, kj, False) for r in range(Q_TILES)])
        return carry

    lax.fori_loop(0, step * Q_TILES, body, 0)

    for r in range(Q_TILES):
        tile = slice(r * TQ, (r + 1) * TQ)
        for h in range(heads):
            cols = slice(h * hd, (h + 1) * hd)
            o_ref[tile, cols] = acc_scr[cols, tile].T.astype(BF16)


def _sb_attn(q, k, vt, batch, seq):
    t = q.shape[0]
    steps = seq // TQ_STEP
    nch = seq // TK
    k3 = k.reshape(batch, seq, SB_W)
    vt4 = vt.reshape(batch, nch, SB_W, TK)
    qrow = lambda b, i: (b * steps + i, 0)
    n_units = SB_HEADS * Q_TILES * (Q_TILES + 1) // 2
    return pl.pallas_call(
        _sb_attn_kernel,
        grid=(batch, steps),
        in_specs=[pl.BlockSpec((TQ_STEP, SB_W), qrow),
                  pl.BlockSpec((None, seq, SB_W), lambda b, i: (b, 0, 0)),
                  pl.BlockSpec((None, nch, SB_W, TK), lambda b, i: (b, 0, 0, 0))],
        out_specs=pl.BlockSpec((TQ_STEP, SB_W), qrow),
        out_shape=jax.ShapeDtypeStruct((t, SB_W), BF16),
        scratch_shapes=[pltpu.VMEM((SB_HEADS, TQ_STEP), F32), pltpu.VMEM((SB_W, TQ_STEP), F32),
                        pltpu.VMEM((n_units, TK, TQ), F32),
                        pltpu.VMEM((n_units, SUM_ROWS, TQ), F32),
                        pltpu.VMEM((n_units, TK - SUM_ROWS, TQ), BF16),
                        pltpu.VMEM((n_units, TK, TQ), BF16)],
        compiler_params=_params(("parallel", "arbitrary"), 48 * 1024 * 1024),
        name="sb_attn",
    )(q, k3, vt4)


MEM_SCALE = MEM_HEAD_DIM ** -0.5 * LOG2E
MEM_W = MEM_HEADS * MEM_HEAD_DIM


def _cross_kernel(h_ref, o_ref, wo_ref, g_ref, wq_ref, kv_ref, wxo_ref, out_ref):
    h1 = h_ref[...] + _dot(o_ref[...], wo_ref[...])
    n = _rmsnorm(h1, g_ref[...]).astype(BF16)
    q = (_dot(n, wq_ref[...]) * MEM_SCALE).astype(BF16)
    head_cols = [slice(hh * MEM_HEAD_DIM, (hh + 1) * MEM_HEAD_DIM) for hh in range(MEM_HEADS)]
    scores = [_dot_nt(q[:, c], kv_ref[:, c]) for c in head_cols]
    probs, denoms = [], []
    for s in scores:
        p = jnp.exp2(s - jnp.max(s, axis=-1, keepdims=True))
        denoms.append(jnp.sum(p, axis=-1, keepdims=True))
        probs.append(p.astype(BF16))
    outs = []
    for c, p, denom in zip(head_cols, probs, denoms):
        v_h = kv_ref[:, MEM_W + c.start:MEM_W + c.stop]
        outs.append((_dot(p, v_h) / denom).astype(BF16))
    o = jnp.concatenate(outs, axis=1)
    out_ref[...] = h1 + _dot(o, wxo_ref[...])


def _cross(h2, o, w_o, w_o_layer, g, w_q, kv, w_xo, layer, seq):
    t, d = h2.shape
    tm = TM
    per_batch = seq // tm
    row = lambda i: (i, 0)
    return pl.pallas_call(
        _cross_kernel,
        grid=(t // tm,),
        in_specs=[pl.BlockSpec((tm, d), row), pl.BlockSpec((tm, o.shape[1]), row),
                  _layer_spec(w_o, w_o_layer), _layer_spec(g, layer), _layer_spec(w_q, layer),
                  pl.BlockSpec((None, None) + kv.shape[2:],
                               lambda i: (layer, i // per_batch, 0, 0)),
                  _layer_spec(w_xo, layer)],
        out_specs=pl.BlockSpec((tm, d), row),
        out_shape=jax.ShapeDtypeStruct((t, d), F32),
        compiler_params=_params(("parallel",), 48 * 1024 * 1024),
        name="cross_attn",
    )(h2, o, w_o, g, w_q, kv, w_xo)


def _mlp_kernel(h_ref, g_ref, win_ref, wout_ref, gf_ref, out_ref, *, final_norm):
    h = h_ref[...]
    n = _rmsnorm(h, g_ref[...]).astype(BF16)
    acc = h
    d_ff = win_ref.shape[1]
    for c in range(d_ff // FF_CHUNK):
        cols = slice(c * FF_CHUNK, (c + 1) * FF_CHUNK)
        u = jnp.maximum(_dot(n, win_ref[:, cols]), 0.0)
        acc = acc + _dot((u * u).astype(BF16), wout_ref[cols, :])
    if final_norm:
        acc = _rmsnorm(acc, gf_ref[...])
    out_ref[...] = acc


def _mlp(h2, g, w_in, w_out, layer, g_final, final_norm):
    t, d = h2.shape
    tm = TM
    row = lambda i: (i, 0)
    return pl.pallas_call(
        functools.partial(_mlp_kernel, final_norm=final_norm),
        grid=(t // tm,),
        in_specs=[pl.BlockSpec((tm, d), row), _layer_spec(g, layer),
                  _layer_spec(w_in, layer), _layer_spec(w_out, layer), _const_spec((1, d))],
        out_specs=pl.BlockSpec((tm, d), row),
        out_shape=jax.ShapeDtypeStruct((t, d), F32),
        compiler_params=_params(("parallel",), 56 * 1024 * 1024),
        name="mlp",
    )(h2, g, w_in, w_out, g_final.reshape(1, d))


def kernel(x, mem, positions, norm_mix, norm_cross, norm_mem, norm_mlp, norm_final,
           mla_w_dkv, mla_g_q, mla_g_kv, mla_w_uq, mla_w_ukv, mla_w_o,
           sb_w_qkv, sb_w_o, xa_w_q, xa_w_kv, xa_w_o, mlp_w_in, mlp_w_out):
    batch, seq, d = x.shape
    depth = norm_mix.shape[0]
    assert depth == 2 and seq % TM == 0 and TM % TK == 0 and TQ == TK and seq % TQ_STEP == 0
    h = x.reshape(batch * seq, d)
    cos, sin = _rope_tables(positions)
    mem_kv = _mem_kv(mem, norm_mem, xa_w_kv.astype(BF16))

    mla_wo, sb_wo = mla_w_o.astype(BF16), sb_w_o.astype(BF16)
    xa_wq, xa_wo = xa_w_q.astype(BF16), xa_w_o.astype(BF16)
    mlp_win, mlp_wout = mlp_w_in.astype(BF16), mlp_w_out.astype(BF16)
    g_cross, g_mlp = _rows(norm_cross), _rows(norm_mlp)

    for i in range(depth):
        j = i // 2
        if i % 2 == 0:
            qn, qp, kn, kpe, vt = _mla_proj(h, norm_mix[i], mla_w_dkv[j], mla_g_q[j],
                                            mla_g_kv[j], mla_w_uq[j], mla_w_ukv[j], cos, sin)
            o = _mla_attn(qn, qp, kn, kpe, vt, batch, seq)
            w_o = mla_wo
        else:
            q, k, vt = _sb_proj(h, norm_mix[i], sb_w_qkv[j])
            o = _sb_attn(q, k, vt, batch, seq)
            w_o = sb_wo
        h = _cross(h, o, w_o, j, g_cross, xa_wq, mem_kv, xa_wo, i, seq)
        h = _mlp(h, g_mlp, mlp_win, mlp_wout, i, norm_final, final_norm=(i == depth - 1))
    return h.reshape(batch, seq, d)
```

```python
import functools
import math

import jax
import jax.numpy as jnp
from jax import lax
from jax.experimental import pallas as pl
from jax.experimental.pallas import tpu as pltpu

F32 = jnp.float32
BF16 = jnp.bfloat16

D_MODEL = 1024
EPS = 1e-6
MLA_HEADS = 8
MLA_Q_LORA = 384
MLA_KV_LORA = 256
MLA_NOPE = 128
MLA_ROPE = 64
MLA_V = 128
ROPE_THETA = 10000.0
SB_HEADS = 8
SB_HEAD_DIM = 128
MEM_HEADS = 4
MEM_HEAD_DIM = 128
NEG_BIG = -1e30
LOG2E = math.log2(math.e)

LANES = 128
V7X_VMEM_BYTES = 64 * 1024 * 1024

TQ = 256
TK = 256
TM = 1024
FF_CHUNK = 1024


def _vmem_limit(nbytes):
    return int(min(max(nbytes, 32 * 1024 * 1024), V7X_VMEM_BYTES - 8 * 1024 * 1024))


def _params(semantics, vmem_bytes):
    return pltpu.CompilerParams(dimension_semantics=semantics,
                                vmem_limit_bytes=_vmem_limit(vmem_bytes))


def _const_spec(shape):
    nd = len(shape)
    return pl.BlockSpec(shape, lambda *_: (0,) * nd)


def _layer_spec(stacked, layer):
    rest = stacked.shape[1:]
    return pl.BlockSpec((None,) + rest, lambda *_: (layer,) + (0,) * len(rest))


def _rows(param):
    return param.reshape(param.shape[0], 1, param.shape[1])


def _rmsnorm(x, g):
    return x * lax.rsqrt(jnp.mean(x * x, axis=-1, keepdims=True) + EPS) * g


def _dot(a, b):
    return jnp.dot(a, b, preferred_element_type=F32)


def _dot_nt(a, b):
    return lax.dot_general(a, b, (((1,), (1,)), ((), ())), preferred_element_type=F32)


ROPE_HALF = MLA_ROPE // 2
ROPE_PER_ROW = LANES // ROPE_HALF


def _exact_bf16_terms(x):
    hi = x.astype(BF16)
    rest = x - hi.astype(F32)
    mid = rest.astype(BF16)
    lo = (rest - mid.astype(F32)).astype(BF16)
    return hi, mid, lo


def _rope_kernel(pos_ref, freq_ref, cos_ref, sin_ref):
    ang = pos_ref[...].astype(F32) * freq_ref[...]
    src = lax.broadcasted_iota(jnp.int32, (LANES, LANES), 0)
    dst = lax.broadcasted_iota(jnp.int32, (LANES, LANES), 1)
    for table, out_ref in ((jnp.cos(ang), cos_ref), (jnp.sin(ang), sin_ref)):
        terms = _exact_bf16_terms(table)
        for j in range(ROPE_PER_ROW):
            spread = jnp.where(src == j * ROPE_HALF + (dst & (ROPE_HALF - 1)), 1.0, 0.0).astype(BF16)
            out_ref[j] = _dot(terms[0], spread) + _dot(terms[1], spread) + _dot(terms[2], spread)


def _rope_tables(positions):
    n_tok = positions.size
    inv_freq = ROPE_THETA ** (-jnp.arange(0, MLA_ROPE, 2, dtype=F32) / MLA_ROPE)
    rows = n_tok // ROPE_PER_ROW
    pos_rep = jnp.repeat(positions.reshape(ROPE_PER_ROW, rows).T, ROPE_HALF, axis=1)
    freq = jnp.tile(inv_freq, ROPE_PER_ROW).reshape(1, LANES)
    rb = 512
    out_spec = pl.BlockSpec((ROPE_PER_ROW, rb, LANES), lambda i: (0, i, 0))
    cos, sin = pl.pallas_call(
        _rope_kernel,
        grid=(rows // rb,),
        in_specs=[pl.BlockSpec((rb, LANES), lambda i: (i, 0)), _const_spec((1, LANES))],
        out_specs=[out_spec] * 2,
        out_shape=[jax.ShapeDtypeStruct((ROPE_PER_ROW, rows, LANES), F32)] * 2,
        compiler_params=_params(("parallel",), 0),
        name="rope_tables",
    )(pos_rep, freq)
    return cos.reshape(n_tok, LANES), sin.reshape(n_tok, LANES)


def _mem_kv_kernel(mem_ref, g_ref, w_ref, kv_ref):
    n = _rmsnorm(mem_ref[...], g_ref[...]).astype(BF16)
    kv_ref[...] = _dot(n, w_ref[...]).astype(BF16)


def _mem_kv(mem, norm_mem, w_kv):
    depth = norm_mem.shape[0]
    b, m, d = mem.shape
    n_out = w_kv.shape[-1]
    return pl.pallas_call(
        _mem_kv_kernel,
        grid=(depth, b),
        in_specs=[pl.BlockSpec((None, m, d), lambda l, i: (i, 0, 0)),
                  pl.BlockSpec((None, 1, d), lambda l, i: (l, 0, 0)),
                  pl.BlockSpec((None, d, n_out), lambda l, i: (l, 0, 0))],
        out_specs=pl.BlockSpec((None, None, m, n_out), lambda l, i: (l, i, 0, 0)),
        out_shape=jax.ShapeDtypeStruct((depth, b, m, n_out), BF16),
        compiler_params=_params(("parallel", "parallel"), 0),
        name="mem_kv",
    )(mem, norm_mem.reshape(depth, 1, d), w_kv)


MLA_QK_SCALE = (MLA_NOPE + MLA_ROPE) ** -0.5 * LOG2E
LAT_Q_END = MLA_Q_LORA
LAT_KV_END = MLA_Q_LORA + MLA_KV_LORA
LAT_EXT = LAT_KV_END + 2 * MLA_ROPE
Q_NOPE_W = MLA_HEADS * MLA_NOPE
Q_ROPE_W = MLA_HEADS * MLA_ROPE


def _mla_proj_kernel(x_ref, g_ref, wdkv_ref, gq_ref, gkv_ref, wuq_ref, wuk_ref, wvt_ref,
                     cos_ref, sin_ref, qn_ref, qp_ref, kn_ref, kpe_ref, vt_ref):
    a = _rmsnorm(x_ref[...], g_ref[...]).astype(BF16)
    lat = _dot(a, wdkv_ref[...])
    c_q = _rmsnorm(lat[:, :LAT_Q_END], gq_ref[...]).astype(BF16)
    c_kv = _rmsnorm(lat[:, LAT_Q_END:LAT_KV_END], gkv_ref[...]).astype(BF16)
    cos = cos_ref[...]
    sin = sin_ref[...]

    slab = lat[:, LAT_KV_END:LAT_EXT]
    rot = slab * cos + pltpu.roll(slab, MLA_ROPE, 1) * sin
    lane = lax.broadcasted_iota(jnp.int32, rot.shape, 1)
    k_even = jnp.where(lane < MLA_ROPE, rot, 0.0)
    k_odd = pltpu.roll(k_even, MLA_ROPE, 1)
    kpe_ref[...] = jnp.concatenate([k_even, k_odd], axis=1).astype(BF16)

    q = _dot(c_q, wuq_ref[...])
    qn_ref[...] = (q[:, :Q_NOPE_W] * MLA_QK_SCALE).astype(BF16)
    for s in range(Q_ROPE_W // LANES):
        lo = Q_NOPE_W + s * LANES
        r = q[:, lo:lo + LANES]
        r_partner = q[:, lo + Q_ROPE_W:lo + Q_ROPE_W + LANES]
        qp_ref[:, s * LANES:(s + 1) * LANES] = (
            (r * cos + r_partner * sin) * MLA_QK_SCALE).astype(BF16)

    kn_ref[...] = _dot(c_kv, wuk_ref[...]).astype(BF16)
    vt = _dot_nt(wvt_ref[...], c_kv)
    for c in range(vt_ref.shape[0]):
        vt_ref[c] = vt[:, c * TK:(c + 1) * TK].astype(BF16)


def _mla_proj(x2, g, w_dkv, g_q, g_kv, w_uq, w_ukv, cos, sin):
    t, d = x2.shape
    h = MLA_HEADS
    kpe0 = LAT_KV_END
    half = MLA_ROPE // 2
    wdkv = jnp.concatenate(
        [w_dkv, -w_dkv[:, kpe0 + half:kpe0 + MLA_ROPE], w_dkv[:, kpe0:kpe0 + half]],
        axis=1).astype(BF16)
    uq = w_uq.reshape(MLA_Q_LORA, h, MLA_NOPE + MLA_ROPE)
    uq_rope = uq[:, :, MLA_NOPE:]
    uq_partner = jnp.concatenate([-uq_rope[:, :, half:], uq_rope[:, :, :half]], axis=-1)
    wuq = jnp.concatenate(
        [uq[:, :, :MLA_NOPE].reshape(MLA_Q_LORA, Q_NOPE_W),
         uq_rope.reshape(MLA_Q_LORA, Q_ROPE_W),
         uq_partner.reshape(MLA_Q_LORA, Q_ROPE_W)], axis=1).astype(BF16)
    ukv = w_ukv.reshape(MLA_KV_LORA, h, MLA_NOPE + MLA_V)
    wuk = ukv[:, :, :MLA_NOPE].reshape(MLA_KV_LORA, h * MLA_NOPE).astype(BF16)
    wvt = ukv[:, :, MLA_NOPE:].reshape(MLA_KV_LORA, h * MLA_V).T.astype(BF16)

    tm = TM
    n_chunks = t // TK
    row = lambda i: (i, 0)
    out_shapes = [
        jax.ShapeDtypeStruct((t, Q_NOPE_W), BF16),
        jax.ShapeDtypeStruct((t, Q_ROPE_W), BF16),
        jax.ShapeDtypeStruct((t, h * MLA_NOPE), BF16),
        jax.ShapeDtypeStruct((t, 2 * LANES), BF16),
        jax.ShapeDtypeStruct((n_chunks, h * MLA_V, TK), BF16),
    ]
    return pl.pallas_call(
        _mla_proj_kernel,
        grid=(t // tm,),
        in_specs=[pl.BlockSpec((tm, d), row), _const_spec((1, d)),
                  _const_spec(wdkv.shape), _const_spec((1, MLA_Q_LORA)),
                  _const_spec((1, MLA_KV_LORA)), _const_spec(wuq.shape),
                  _const_spec(wuk.shape), _const_spec(wvt.shape),
                  pl.BlockSpec((tm, LANES), row), pl.BlockSpec((tm, LANES), row)],
        out_specs=[pl.BlockSpec((tm, Q_NOPE_W), row), pl.BlockSpec((tm, Q_ROPE_W), row),
                   pl.BlockSpec((tm, h * MLA_NOPE), row), pl.BlockSpec((tm, 2 * LANES), row),
                   pl.BlockSpec((tm // TK, h * MLA_V, TK), lambda i: (i, 0, 0))],
        out_shape=out_shapes,
        compiler_params=_params(("parallel",), 48 * 1024 * 1024),
        name="mla_proj",
    )(x2, g.reshape(1, d), wdkv, g_q.reshape(1, -1), g_kv.reshape(1, -1), wuq, wuk, wvt,
      cos, sin)


Q_TILES = 2
SUM_ROWS = 16
TQ_STEP = Q_TILES * TQ


def _diag_groups(step, mixer_order):
    groups = []
    for r in range(Q_TILES):
        own = [(r, step * Q_TILES + c, c == r) for c in range(r + 1)]
        groups += own if mixer_order == "any" else own[::-1]
    return groups


def _key_offset(kj):
    off = kj * TK
    return off if isinstance(off, int) else pl.multiple_of(off, TK)


def _mla_attn_kernel(qn_ref, qp_ref, kn_ref, kpe_ref, vt_ref, o_ref, m_scr, l_scr, acc_scr, s_scr):
    step = pl.program_id(1)
    heads, hd = MLA_HEADS, MLA_NOPE
    m_scr[...] = jnp.full(m_scr.shape, NEG_BIG, F32)
    l_scr[...] = jnp.zeros(l_scr.shape, F32)
    acc_scr[...] = jnp.zeros(acc_scr.shape, F32)

    def process(groups):
        units = [(r, kj, masked, h) for r, kj, masked in groups for h in range(heads)]
        key = lax.broadcasted_iota(jnp.int32, (TK, TQ), 0)
        qry = lax.broadcasted_iota(jnp.int32, (TK, TQ), 1)
        stats = []
        for u, (r, kj, masked, h) in enumerate(units):
            koff = _key_offset(kj)
            tile = slice(r * TQ, (r + 1) * TQ)
            cols = slice(h * hd, (h + 1) * hd)
            pair = slice((h // 2) * LANES, (h // 2 + 1) * LANES)
            par = slice((h % 2) * LANES, (h % 2 + 1) * LANES)
            q_h = jnp.concatenate([qn_ref[tile, cols], qp_ref[tile, pair]], axis=1)
            k_h = jnp.concatenate([kn_ref[pl.ds(koff, TK), cols],
                                   kpe_ref[pl.ds(koff, TK), par]], axis=1)
            s = _dot_nt(k_h, q_h)
            if masked:
                s = jnp.where(key <= qry, s, NEG_BIG)
            s_scr[u] = s
            m_prev = m_scr[h:h + 1, tile]
            m_new = jnp.maximum(m_prev, jnp.max(s, axis=0, keepdims=True))
            m_scr[h:h + 1, tile] = m_new
            stats.append((m_new, jnp.exp2(m_prev - m_new)))
        p_bf = [jnp.exp2(s_scr[u] - stats[u][0]).astype(BF16) for u in range(len(units))]
        ones = jnp.ones((SUM_ROWS, TK), BF16)
        for u, (r, kj, masked, h) in enumerate(units):
            tile = slice(r * TQ, (r + 1) * TQ)
            cols = slice(h * hd, (h + 1) * hd)
            alpha = stats[u][1]
            v_ext = jnp.concatenate([vt_ref[kj, cols, :], ones], axis=0)
            pv = _dot(v_ext, p_bf[u])
            acc_scr[cols, tile] = alpha * acc_scr[cols, tile] + pv[:hd, :]
            l_scr[h:h + 1, tile] = alpha * l_scr[h:h + 1, tile] + pv[hd:hd + 1, :]

    process(_diag_groups(step, "any"))

    def body(kj, carry):
        process([(r, kj, False) for r in range(Q_TILES)])
        return carry

    lax.fori_loop(0, step * Q_TILES, body, 0)

    for r in range(Q_TILES):
        tile = slice(r * TQ, (r + 1) * TQ)
        for h in range(heads):
            cols = slice(h * hd, (h + 1) * hd)
            o_t = acc_scr[cols, tile] / l_scr[h:h + 1, tile]
            o_ref[tile, cols] = o_t.T.astype(BF16)


def _mla_attn(qn, qp, kn, kpe, vt, batch, seq):
    t = qn.shape[0]
    steps = seq // TQ_STEP
    nch = seq // TK
    hv = MLA_HEADS * MLA_V
    kn3 = kn.reshape(batch, seq, -1)
    kpe3 = kpe.reshape(batch, seq, -1)
    vt4 = vt.reshape(batch, nch, hv, TK)
    qrow = lambda b, i: (b * steps + i, 0)
    perb = lambda b, i: (b, 0, 0)
    n_units = MLA_HEADS * Q_TILES * (Q_TILES + 1) // 2
    return pl.pallas_call(
        _mla_attn_kernel,
        grid=(batch, steps),
        in_specs=[pl.BlockSpec((TQ_STEP, qn.shape[1]), qrow),
                  pl.BlockSpec((TQ_STEP, qp.shape[1]), qrow),
                  pl.BlockSpec((None, seq, kn3.shape[2]), perb),
                  pl.BlockSpec((None, seq, kpe3.shape[2]), perb),
                  pl.BlockSpec((None, nch, hv, TK), lambda b, i: (b, 0, 0, 0))],
        out_specs=pl.BlockSpec((TQ_STEP, hv), qrow),
        out_shape=jax.ShapeDtypeStruct((t, hv), BF16),
        scratch_shapes=[pltpu.VMEM((MLA_HEADS, TQ_STEP), F32), pltpu.VMEM((MLA_HEADS, TQ_STEP), F32),
                        pltpu.VMEM((hv, TQ_STEP), F32), pltpu.VMEM((n_units, TK, TQ), F32)],
        compiler_params=_params(("parallel", "arbitrary"), 48 * 1024 * 1024),
        name="mla_attn",
    )(qn, qp, kn3, kpe3, vt4)


SB_SCALE = SB_HEAD_DIM ** -0.5 * LOG2E
SB_W = SB_HEADS * SB_HEAD_DIM


def _sb_proj_kernel(x_ref, g_ref, wqk_ref, wvt_ref, q_ref, k_ref, vt_ref):
    a = _rmsnorm(x_ref[...], g_ref[...]).astype(BF16)
    qk = _dot(a, wqk_ref[...])
    q_ref[...] = (qk[:, :SB_W] * SB_SCALE).astype(BF16)
    k_ref[...] = qk[:, SB_W:].astype(BF16)
    vt = _dot_nt(wvt_ref[...], a)
    for c in range(vt_ref.shape[0]):
        vt_ref[c] = vt[:, c * TK:(c + 1) * TK].astype(BF16)


def _sb_proj(x2, g, w_qkv):
    t, d = x2.shape
    wqk = w_qkv[:, :2 * SB_W].astype(BF16)
    wvt = w_qkv[:, 2 * SB_W:].T.astype(BF16)
    tm = TM
    row = lambda i: (i, 0)
    return pl.pallas_call(
        _sb_proj_kernel,
        grid=(t // tm,),
        in_specs=[pl.BlockSpec((tm, d), row), _const_spec((1, d)),
                  _const_spec(wqk.shape), _const_spec(wvt.shape)],
        out_specs=[pl.BlockSpec((tm, SB_W), row), pl.BlockSpec((tm, SB_W), row),
                   pl.BlockSpec((tm // TK, SB_W, TK), lambda i: (i, 0, 0))],
        out_shape=[jax.ShapeDtypeStruct((t, SB_W), BF16), jax.ShapeDtypeStruct((t, SB_W), BF16),
                   jax.ShapeDtypeStruct((t // TK, SB_W, TK), BF16)],
        compiler_params=_params(("parallel",), 48 * 1024 * 1024),
        name="sb_proj",
    )(x2, g.reshape(1, d), wqk, wvt)


SOFTPLUS_LINEAR_FROM = 64.0


def _sb_attn_kernel(q_ref, k_ref, vt_ref, o_ref, carry_scr, acc_scr, lb_scr, top_scr, sp_scr, a_scr):
    step = pl.program_id(1)
    heads, hd = SB_HEADS, SB_HEAD_DIM
    row = lax.broadcasted_iota(jnp.int32, (TK, TK), 0)
    col = lax.broadcasted_iota(jnp.int32, (TK, TK), 1)
    suffix = jnp.where((col > row) | (col == 0), 1.0, 0.0).astype(BF16)
    first_row = lax.broadcasted_iota(jnp.int32, (SUM_ROWS, TQ), 0) == 0
    carry_scr[...] = jnp.zeros(carry_scr.shape, F32)
    acc_scr[...] = jnp.zeros(acc_scr.shape, F32)

    def process(groups):
        units = [(r, kj, masked, h) for r, kj, masked in groups for h in range(heads)]
        key = lax.broadcasted_iota(jnp.int32, (TK, TQ), 0)
        qry = lax.broadcasted_iota(jnp.int32, (TK, TQ), 1)
        for u, (r, kj, masked, h) in enumerate(units):
            koff = _key_offset(kj)
            tile = slice(r * TQ, (r + 1) * TQ)
            cols = slice(h * hd, (h + 1) * hd)
            z = _dot_nt(k_ref[pl.ds(koff, TK), cols], q_ref[tile, cols])
            sp = jnp.where(z > SOFTPLUS_LINEAR_FROM, z, jnp.log2(1.0 + jnp.exp2(z)))
            lb_scr[u] = z - sp
            if masked:
                sp = jnp.where(key < qry, sp, 0.0)
            top_scr[u] = sp[:SUM_ROWS, :]
            sp_scr[u] = sp[SUM_ROWS:, :].astype(BF16)
        for u, (r, kj, masked, h) in enumerate(units):
            tile = slice(r * TQ, (r + 1) * TQ)
            seen = carry_scr[h:h + 1, tile]
            top = top_scr[u]
            rhs = jnp.concatenate(
                [jnp.where(first_row, seen, top).astype(BF16), sp_scr[u]], axis=0)
            later = _dot(suffix, rhs)
            seen_as_summed = seen.astype(BF16).astype(F32)
            carry_scr[h:h + 1, tile] = seen + top[0:1, :] + (later[0:1, :] - seen_as_summed)
            a = jnp.exp2(lb_scr[u] - later)
            if masked:
                a = jnp.where(key < qry, a, 0.0)
            a_scr[u] = a.astype(BF16)
        for u, (r, kj, masked, h) in enumerate(units):
            tile = slice(r * TQ, (r + 1) * TQ)
            cols = slice(h * hd, (h + 1) * hd)
            acc_scr[cols, tile] = acc_scr[cols, tile] + _dot(vt_ref[kj, cols, :], a_scr[u])

    process(_diag_groups(step, "right_to_left"))

    def body(i, carry):
        kj = step * Q_TILES - 1 - i
        process([(r, kj, False) for r in range(Q_TILES)])
        return carry

    lax.fori_loop(0, step * Q_TILES, body, 0)

    for r in range(Q_TILES):
        tile = slice(r * TQ, (r + 1) * TQ)
        for h in range(heads):
            cols = slice(h * hd, (h + 1) * hd)
            o_ref[tile, cols] = acc_scr[cols, tile].T.astype(BF16)


def _sb_attn(q, k, vt, batch, seq):
    t = q.shape[0]
    steps = seq // TQ_STEP
    nch = seq // TK
    k3 = k.reshape(batch, seq, SB_W)
    vt4 = vt.reshape(batch, nch, SB_W, TK)
    qrow = lambda b, i: (b * steps + i, 0)
    n_units = SB_HEADS * Q_TILES * (Q_TILES + 1) // 2
    return pl.pallas_call(
        _sb_attn_kernel,
        grid=(batch, steps),
        in_specs=[pl.BlockSpec((TQ_STEP, SB_W), qrow),
                  pl.BlockSpec((None, seq, SB_W), lambda b, i: (b, 0, 0)),
                  pl.BlockSpec((None, nch, SB_W, TK), lambda b, i: (b, 0, 0, 0))],
        out_specs=pl.BlockSpec((TQ_STEP, SB_W), qrow),
        out_shape=jax.ShapeDtypeStruct((t, SB_W), BF16),
        scratch_shapes=[pltpu.VMEM((SB_HEADS, TQ_STEP), F32), pltpu.VMEM((SB_W, TQ_STEP), F32),
                        pltpu.VMEM((n_units, TK, TQ), F32),
                        pltpu.VMEM((n_units, SUM_ROWS, TQ), F32),
                        pltpu.VMEM((n_units, TK - SUM_ROWS, TQ), BF16),
                        pltpu.VMEM((n_units, TK, TQ), BF16)],
        compiler_params=_params(("parallel", "arbitrary"), 48 * 1024 * 1024),
        name="sb_attn",
    )(q, k3, vt4)


MEM_SCALE = MEM_HEAD_DIM ** -0.5 * LOG2E
MEM_W = MEM_HEADS * MEM_HEAD_DIM


def _cross_kernel(h_ref, o_ref, wo_ref, g_ref, wq_ref, kv_ref, wxo_ref, out_ref):
    h1 = h_ref[...] + _dot(o_ref[...], wo_ref[...])
    n = _rmsnorm(h1, g_ref[...]).astype(BF16)
    q = (_dot(n, wq_ref[...]) * MEM_SCALE).astype(BF16)
    head_cols = [slice(hh * MEM_HEAD_DIM, (hh + 1) * MEM_HEAD_DIM) for hh in range(MEM_HEADS)]
    scores = [_dot_nt(q[:, c], kv_ref[:, c]) for c in head_cols]
    probs, denoms = [], []
    for s in scores:
        p = jnp.exp2(s - jnp.max(s, axis=-1, keepdims=True))
        denoms.append(jnp.sum(p, axis=-1, keepdims=True))
        probs.append(p.astype(BF16))
    outs = []
    for c, p, denom in zip(head_cols, probs, denoms):
        v_h = kv_ref[:, MEM_W + c.start:MEM_W + c.stop]
        outs.append((_dot(p, v_h) / denom).astype(BF16))
    o = jnp.concatenate(outs, axis=1)
    out_ref[...] = h1 + _dot(o, wxo_ref[...])


def _cross(h2, o, w_o, w_o_layer, g, w_q, kv, w_xo, layer, seq):
    t, d = h2.shape
    tm = TM
    per_batch = seq // tm
    row = lambda i: (i, 0)
    return pl.pallas_call(
        _cross_kernel,
        grid=(t // tm,),
        in_specs=[pl.BlockSpec((tm, d), row), pl.BlockSpec((tm, o.shape[1]), row),
                  _layer_spec(w_o, w_o_layer), _layer_spec(g, layer), _layer_spec(w_q, layer),
                  pl.BlockSpec((None, None) + kv.shape[2:],
                               lambda i: (layer, i // per_batch, 0, 0)),
                  _layer_spec(w_xo, layer)],
        out_specs=pl.BlockSpec((tm, d), row),
        out_shape=jax.ShapeDtypeStruct((t, d), F32),
        compiler_params=_params(("parallel",), 48 * 1024 * 1024),
        name="cross_attn",
    )(h2, o, w_o, g, w_q, kv, w_xo)


def _mlp_kernel(h_ref, g_ref, win_ref, wout_ref, gf_ref, out_ref, *, final_norm):
    h = h_ref[...]
    n = _rmsnorm(h, g_ref[...]).astype(BF16)
    acc = h
    d_ff = win_ref.shape[1]
    for c in range(d_ff // FF_CHUNK):
        cols = slice(c * FF_CHUNK, (c + 1) * FF_CHUNK)
        u = jnp.maximum(_dot(n, win_ref[:, cols]), 0.0)
        acc = acc + _dot((u * u).astype(BF16), wout_ref[cols, :])
    if final_norm:
        acc = _rmsnorm(acc, gf_ref[...])
    out_ref[...] = acc


def _mlp(h2, g, w_in, w_out, layer, g_final, final_norm):
    t, d = h2.shape
    tm = TM
    row = lambda i: (i, 0)
    return pl.pallas_call(
        functools.partial(_mlp_kernel, final_norm=final_norm),
        grid=(t // tm,),
        in_specs=[pl.BlockSpec((tm, d), row), _layer_spec(g, layer),
                  _layer_spec(w_in, layer), _layer_spec(w_out, layer), _const_spec((1, d))],
        out_specs=pl.BlockSpec((tm, d), row),
        out_shape=jax.ShapeDtypeStruct((t, d), F32),
        compiler_params=_params(("parallel",), 56 * 1024 * 1024),
        name="mlp",
    )(h2, g, w_in, w_out, g_final.reshape(1, d))


def kernel(x, mem, positions, norm_mix, norm_cross, norm_mem, norm_mlp, norm_final,
           mla_w_dkv, mla_g_q, mla_g_kv, mla_w_uq, mla_w_ukv, mla_w_o,
           sb_w_qkv, sb_w_o, xa_w_q, xa_w_kv, xa_w_o, mlp_w_in, mlp_w_out):
    batch, seq, d = x.shape
    depth = norm_mix.shape[0]
    assert depth == 2 and seq % TM == 0 and TM % TK == 0 and TQ == TK and seq % TQ_STEP == 0
    h = x.reshape(batch * seq, d)
    cos, sin = _rope_tables(positions)
    mem_kv = _mem_kv(mem, norm_mem, xa_w_kv.astype(BF16))

    mla_wo, sb_wo = mla_w_o.astype(BF16), sb_w_o.astype(BF16)
    xa_wq, xa_wo = xa_w_q.astype(BF16), xa_w_o.astype(BF16)
    mlp_win, mlp_wout = mlp_w_in.astype(BF16), mlp_w_out.astype(BF16)
    g_cross, g_mlp = _rows(norm_cross), _rows(norm_mlp)

    for i in range(depth):
        j = i // 2
        if i % 2 == 0:
            qn, qp, kn, kpe, vt = _mla_proj(h, norm_mix[i], mla_w_dkv[j], mla_g_q[j],
                                            mla_g_kv[j], mla_w_uq[j], mla_w_ukv[j], cos, sin)
            o = _mla_attn(qn, qp, kn, kpe, vt, batch, seq)
            w_o = mla_wo
        else:
            q, k, vt = _sb_proj(h, norm_mix[i], sb_w_qkv[j])
            o = _sb_attn(q, k, vt, batch, seq)
            w_o = sb_wo
        h = _cross(h, o, w_o, j, g_cross, xa_wq, mem_kv, xa_wo, i, seq)
        h = _mlp(h, g_mlp, mlp_win, mlp_wout, i, norm_final, final_norm=(i == depth - 1))
    return h.reshape(batch, seq, d)
```

```python
import functools
import math

import jax
import jax.numpy as jnp
from jax import lax
from jax.experimental import pallas as pl
from jax.experimental.pallas import tpu as pltpu

F32 = jnp.float32
BF16 = jnp.bfloat16

D_MODEL = 1024
EPS = 1e-6
MLA_HEADS = 8
MLA_Q_LORA = 384
MLA_KV_LORA = 256
MLA_NOPE = 128
MLA_ROPE = 64
MLA_V = 128
ROPE_THETA = 10000.0
SB_HEADS = 8
SB_HEAD_DIM = 128
MEM_HEADS = 4
MEM_HEAD_DIM = 128
NEG_BIG = -1e30
LOG2E = math.log2(math.e)

LANES = 128
V7X_VMEM_BYTES = 64 * 1024 * 1024

TQ = 256
TK = 256
TM = 1024
FF_CHUNK = 1024


def _vmem_limit(nbytes):
    return int(min(max(nbytes, 32 * 1024 * 1024), V7X_VMEM_BYTES - 8 * 1024 * 1024))


def _params(semantics, vmem_bytes):
    return pltpu.CompilerParams(dimension_semantics=semantics,
                                vmem_limit_bytes=_vmem_limit(vmem_bytes))


def _const_spec(shape):
    nd = len(shape)
    return pl.BlockSpec(shape, lambda *_: (0,) * nd)


def _layer_spec(stacked, layer):
    rest = stacked.shape[1:]
    return pl.BlockSpec((None,) + rest, lambda *_: (layer,) + (0,) * len(rest))


def _rows(param):
    return param.reshape(param.shape[0], 1, param.shape[1])


def _rmsnorm(x, g):
    return x * lax.rsqrt(jnp.mean(x * x, axis=-1, keepdims=True) + EPS) * g


def _dot(a, b):
    return jnp.dot(a, b, preferred_element_type=F32)


def _dot_nt(a, b):
    return lax.dot_general(a, b, (((1,), (1,)), ((), ())), preferred_element_type=F32)


ROPE_HALF = MLA_ROPE // 2
ROPE_PER_ROW = LANES // ROPE_HALF


def _exact_bf16_terms(x):
    hi = x.astype(BF16)
    rest = x - hi.astype(F32)
    mid = rest.astype(BF16)
    lo = (rest - mid.astype(F32)).astype(BF16)
    return hi, mid, lo


def _rope_kernel(pos_ref, freq_ref, cos_ref, sin_ref):
    ang = pos_ref[...].astype(F32) * freq_ref[...]
    src = lax.broadcasted_iota(jnp.int32, (LANES, LANES), 0)
    dst = lax.broadcasted_iota(jnp.int32, (LANES, LANES), 1)
    for table, out_ref in ((jnp.cos(ang), cos_ref), (jnp.sin(ang), sin_ref)):
        terms = _exact_bf16_terms(table)
        for j in range(ROPE_PER_ROW):
            spread = jnp.where(src == j * ROPE_HALF + (dst & (ROPE_HALF - 1)), 1.0, 0.0).astype(BF16)
            out_ref[j] = _dot(terms[0], spread) + _dot(terms[1], spread) + _dot(terms[2], spread)


def _rope_tables(positions):
    n_tok = positions.size
    inv_freq = ROPE_THETA ** (-jnp.arange(0, MLA_ROPE, 2, dtype=F32) / MLA_ROPE)
    rows = n_tok // ROPE_PER_ROW
    pos_rep = jnp.repeat(positions.reshape(ROPE_PER_ROW, rows).T, ROPE_HALF, axis=1)
    freq = jnp.tile(inv_freq, ROPE_PER_ROW).reshape(1, LANES)
    rb = 512
    out_spec = pl.BlockSpec((ROPE_PER_ROW, rb, LANES), lambda i: (0, i, 0))
    cos, sin = pl.pallas_call(
        _rope_kernel,
        grid=(rows // rb,),
        in_specs=[pl.BlockSpec((rb, LANES), lambda i: (i, 0)), _const_spec((1, LANES))],
        out_specs=[out_spec] * 2,
        out_shape=[jax.ShapeDtypeStruct((ROPE_PER_ROW, rows, LANES), F32)] * 2,
        compiler_params=_params(("parallel",), 0),
        name="rope_tables",
    )(pos_rep, freq)
    return cos.reshape(n_tok, LANES), sin.reshape(n_tok, LANES)


def _mem_kv_kernel(mem_ref, g_ref, w_ref, kv_ref):
    n = _rmsnorm(mem_ref[...], g_ref[...]).astype(BF16)
    kv_ref[...] = _dot(n, w_ref[...]).astype(BF16)


def _mem_kv(mem, norm_mem, w_kv):
    depth = norm_mem.shape[0]
    b, m, d = mem.shape
    n_out = w_kv.shape[-1]
    return pl.pallas_call(
        _mem_kv_kernel,
        grid=(depth, b),
        in_specs=[pl.BlockSpec((None, m, d), lambda l, i: (i, 0, 0)),
                  pl.BlockSpec((None, 1, d), lambda l, i: (l, 0, 0)),
                  pl.BlockSpec((None, d, n_out), lambda l, i: (l, 0, 0))],
        out_specs=pl.BlockSpec((None, None, m, n_out), lambda l, i: (l, i, 0, 0)),
        out_shape=jax.ShapeDtypeStruct((depth, b, m, n_out), BF16),
        compiler_params=_params(("parallel", "parallel"), 0),
        name="mem_kv",
    )(mem, norm_mem.reshape(depth, 1, d), w_kv)


MLA_QK_SCALE = (MLA_NOPE + MLA_ROPE) ** -0.5 * LOG2E
LAT_Q_END = MLA_Q_LORA
LAT_KV_END = MLA_Q_LORA + MLA_KV_LORA
LAT_EXT = LAT_KV_END + 2 * MLA_ROPE
Q_NOPE_W = MLA_HEADS * MLA_NOPE
Q_ROPE_W = MLA_HEADS * MLA_ROPE


def _mla_proj_kernel(x_ref, g_ref, wdkv_ref, gq_ref, gkv_ref, wuq_ref, wuk_ref, wvt_ref,
                     cos_ref, sin_ref, qn_ref, qp_ref, kn_ref, kpe_ref, vt_ref):
    a = _rmsnorm(x_ref[...], g_ref[...]).astype(BF16)
    lat = _dot(a, wdkv_ref[...])
    c_q = _rmsnorm(lat[:, :LAT_Q_END], gq_ref[...]).astype(BF16)
    c_kv = _rmsnorm(lat[:, LAT_Q_END:LAT_KV_END], gkv_ref[...]).astype(BF16)
    cos = cos_ref[...]
    sin = sin_ref[...]

    slab = lat[:, LAT_KV_END:LAT_EXT]
    rot = slab * cos + pltpu.roll(slab, MLA_ROPE, 1) * sin
    lane = lax.broadcasted_iota(jnp.int32, rot.shape, 1)
    k_even = jnp.where(lane < MLA_ROPE, rot, 0.0)
    k_odd = pltpu.roll(k_even, MLA_ROPE, 1)
    kpe_ref[...] = jnp.concatenate([k_even, k_odd], axis=1).astype(BF16)

    q = _dot(c_q, wuq_ref[...])
    qn_ref[...] = (q[:, :Q_NOPE_W] * MLA_QK_SCALE).astype(BF16)
    for s in range(Q_ROPE_W // LANES):
        lo = Q_NOPE_W + s * LANES
        r = q[:, lo:lo + LANES]
        r_partner = q[:, lo + Q_ROPE_W:lo + Q_ROPE_W + LANES]
        qp_ref[:, s * LANES:(s + 1) * LANES] = (
            (r * cos + r_partner * sin) * MLA_QK_SCALE).astype(BF16)

    kn_ref[...] = _dot(c_kv, wuk_ref[...]).astype(BF16)
    vt = _dot_nt(wvt_ref[...], c_kv)
    for c in range(vt_ref.shape[0]):
        vt_ref[c] = vt[:, c * TK:(c + 1) * TK].astype(BF16)


def _mla_proj(x2, g, w_dkv, g_q, g_kv, w_uq, w_ukv, cos, sin):
    t, d = x2.shape
    h = MLA_HEADS
    kpe0 = LAT_KV_END
    half = MLA_ROPE // 2
    wdkv = jnp.concatenate(
        [w_dkv, -w_dkv[:, kpe0 + half:kpe0 + MLA_ROPE], w_dkv[:, kpe0:kpe0 + half]],
        axis=1).astype(BF16)
    uq = w_uq.reshape(MLA_Q_LORA, h, MLA_NOPE + MLA_ROPE)
    uq_rope = uq[:, :, MLA_NOPE:]
    uq_partner = jnp.concatenate([-uq_rope[:, :, half:], uq_rope[:, :, :half]], axis=-1)
    wuq = jnp.concatenate(
        [uq[:, :, :MLA_NOPE].reshape(MLA_Q_LORA, Q_NOPE_W),
         uq_rope.reshape(MLA_Q_LORA, Q_ROPE_W),
         uq_partner.reshape(MLA_Q_LORA, Q_ROPE_W)], axis=1).astype(BF16)
    ukv = w_ukv.reshape(MLA_KV_LORA, h, MLA_NOPE + MLA_V)
    wuk = ukv[:, :, :MLA_NOPE].reshape(MLA_KV_LORA, h * MLA_NOPE).astype(BF16)
    wvt = ukv[:, :, MLA_NOPE:].reshape(MLA_KV_LORA, h * MLA_V).T.astype(BF16)

    tm = TM
    n_chunks = t // TK
    row = lambda i: (i, 0)
    out_shapes = [
        jax.ShapeDtypeStruct((t, Q_NOPE_W), BF16),
        jax.ShapeDtypeStruct((t, Q_ROPE_W), BF16),
        jax.ShapeDtypeStruct((t, h * MLA_NOPE), BF16),
        jax.ShapeDtypeStruct((t, 2 * LANES), BF16),
        jax.ShapeDtypeStruct((n_chunks, h * MLA_V, TK), BF16),
    ]
    return pl.pallas_call(
        _mla_proj_kernel,
        grid=(t // tm,),
        in_specs=[pl.BlockSpec((tm, d), row), _const_spec((1, d)),
                  _const_spec(wdkv.shape), _const_spec((1, MLA_Q_LORA)),
                  _const_spec((1, MLA_KV_LORA)), _const_spec(wuq.shape),
                  _const_spec(wuk.shape), _const_spec(wvt.shape),
                  pl.BlockSpec((tm, LANES), row), pl.BlockSpec((tm, LANES), row)],
        out_specs=[pl.BlockSpec((tm, Q_NOPE_W), row), pl.BlockSpec((tm, Q_ROPE_W), row),
                   pl.BlockSpec((tm, h * MLA_NOPE), row), pl.BlockSpec((tm, 2 * LANES), row),
                   pl.BlockSpec((tm // TK, h * MLA_V, TK), lambda i: (i, 0, 0))],
        out_shape=out_shapes,
        compiler_params=_params(("parallel",), 48 * 1024 * 1024),
        name="mla_proj",
    )(x2, g.reshape(1, d), wdkv, g_q.reshape(1, -1), g_kv.reshape(1, -1), wuq, wuk, wvt,
      cos, sin)


Q_TILES = 2
SUM_ROWS = 16
TQ_STEP = Q_TILES * TQ
CHUNKS_PER_ITER = 2
GROUPS_PER_BLOCK = max(Q_TILES * (Q_TILES + 1) // 2, Q_TILES * CHUNKS_PER_ITER)


def _diag_groups(step, mixer_order):
    groups = []
    for r in range(Q_TILES):
        own = [(r, step * Q_TILES + c, c == r) for c in range(r + 1)]
        groups += own if mixer_order == "any" else own[::-1]
    return groups


def _key_offset(kj):
    off = kj * TK
    return off if isinstance(off, int) else pl.multiple_of(off, TK)


def _mla_attn_kernel(qn_ref, qp_ref, kn_ref, kpe_ref, vt_ref, o_ref, m_scr, l_scr, acc_scr, s_scr):
    step = pl.program_id(1)
    heads, hd = MLA_HEADS, MLA_NOPE
    m_scr[...] = jnp.full(m_scr.shape, NEG_BIG, F32)
    l_scr[...] = jnp.zeros(l_scr.shape, F32)
    acc_scr[...] = jnp.zeros(acc_scr.shape, F32)

    def process(groups):
        units = [(r, kj, masked, h) for r, kj, masked in groups for h in range(heads)]
        key = lax.broadcasted_iota(jnp.int32, (TK, TQ), 0)
        qry = lax.broadcasted_iota(jnp.int32, (TK, TQ), 1)
        stats = []
        for u, (r, kj, masked, h) in enumerate(units):
            koff = _key_offset(kj)
            tile = slice(r * TQ, (r + 1) * TQ)
            cols = slice(h * hd, (h + 1) * hd)
            pair = slice((h // 2) * LANES, (h // 2 + 1) * LANES)
            par = slice((h % 2) * LANES, (h % 2 + 1) * LANES)
            q_h = jnp.concatenate([qn_ref[tile, cols], qp_ref[tile, pair]], axis=1)
            k_h = jnp.concatenate([kn_ref[pl.ds(koff, TK), cols],
                                   kpe_ref[pl.ds(koff, TK), par]], axis=1)
            s = _dot_nt(k_h, q_h)
            if masked:
                s = jnp.where(key <= qry, s, NEG_BIG)
            s_scr[u] = s
            m_prev = m_scr[h:h + 1, tile]
            m_new = jnp.maximum(m_prev, jnp.max(s, axis=0, keepdims=True))
            m_scr[h:h + 1, tile] = m_new
            stats.append((m_new, jnp.exp2(m_prev - m_new)))
        p_bf = [jnp.exp2(s_scr[u] - stats[u][0]).astype(BF16) for u in range(len(units))]
        ones = jnp.ones((SUM_ROWS, TK), BF16)
        for u, (r, kj, masked, h) in enumerate(units):
            tile = slice(r * TQ, (r + 1) * TQ)
            cols = slice(h * hd, (h + 1) * hd)
            alpha = stats[u][1]
            v_ext = jnp.concatenate([vt_ref[kj, cols, :], ones], axis=0)
            pv = _dot(v_ext, p_bf[u])
            acc_scr[cols, tile] = alpha * acc_scr[cols, tile] + pv[:hd, :]
            l_scr[h:h + 1, tile] = alpha * l_scr[h:h + 1, tile] + pv[hd:hd + 1, :]

    process(_diag_groups(step, "any"))

    def body(it, carry):
        process([(r, it * CHUNKS_PER_ITER + c, False)
                 for c in range(CHUNKS_PER_ITER) for r in range(Q_TILES)])
        return carry

    lax.fori_loop(0, step * Q_TILES // CHUNKS_PER_ITER, body, 0)

    for r in range(Q_TILES):
        tile = slice(r * TQ, (r + 1) * TQ)
        for h in range(heads):
            cols = slice(h * hd, (h + 1) * hd)
            o_t = acc_scr[cols, tile] / l_scr[h:h + 1, tile]
            o_ref[tile, cols] = o_t.T.astype(BF16)


def _mla_attn(qn, qp, kn, kpe, vt, batch, seq):
    t = qn.shape[0]
    steps = seq // TQ_STEP
    nch = seq // TK
    hv = MLA_HEADS * MLA_V
    kn3 = kn.reshape(batch, seq, -1)
    kpe3 = kpe.reshape(batch, seq, -1)
    vt4 = vt.reshape(batch, nch, hv, TK)
    qrow = lambda b, i: (b * steps + i, 0)
    perb = lambda b, i: (b, 0, 0)
    n_units = MLA_HEADS * GROUPS_PER_BLOCK
    return pl.pallas_call(
        _mla_attn_kernel,
        grid=(batch, steps),
        in_specs=[pl.BlockSpec((TQ_STEP, qn.shape[1]), qrow),
                  pl.BlockSpec((TQ_STEP, qp.shape[1]), qrow),
                  pl.BlockSpec((None, seq, kn3.shape[2]), perb),
                  pl.BlockSpec((None, seq, kpe3.shape[2]), perb),
                  pl.BlockSpec((None, nch, hv, TK), lambda b, i: (b, 0, 0, 0))],
        out_specs=pl.BlockSpec((TQ_STEP, hv), qrow),
        out_shape=jax.ShapeDtypeStruct((t, hv), BF16),
        scratch_shapes=[pltpu.VMEM((MLA_HEADS, TQ_STEP), F32), pltpu.VMEM((MLA_HEADS, TQ_STEP), F32),
                        pltpu.VMEM((hv, TQ_STEP), F32), pltpu.VMEM((n_units, TK, TQ), F32)],
        compiler_params=_params(("parallel", "arbitrary"), 48 * 1024 * 1024),
        name="mla_attn",
    )(qn, qp, kn3, kpe3, vt4)


SB_SCALE = SB_HEAD_DIM ** -0.5 * LOG2E
SB_W = SB_HEADS * SB_HEAD_DIM


def _sb_proj_kernel(x_ref, g_ref, w_ref, q_ref, k_ref, vt_ref, wqk_scr, wvt_scr):
    @pl.when(pl.program_id(0) == 0)
    def _():
        wqk_scr[...] = w_ref[:, :2 * SB_W].astype(BF16)
        wvt_scr[...] = w_ref[:, 2 * SB_W:].T.astype(BF16)

    a = _rmsnorm(x_ref[...], g_ref[...]).astype(BF16)
    qk = _dot(a, wqk_scr[...])
    q_ref[...] = (qk[:, :SB_W] * SB_SCALE).astype(BF16)
    k_ref[...] = qk[:, SB_W:].astype(BF16)
    vt = _dot_nt(wvt_scr[...], a)
    for c in range(vt_ref.shape[0]):
        vt_ref[c] = vt[:, c * TK:(c + 1) * TK].astype(BF16)


def _sb_proj(x2, g, w_qkv):
    t, d = x2.shape
    tm = TM
    row = lambda i: (i, 0)
    once = pl.BlockSpec(w_qkv.shape, lambda i: (0, 0), pipeline_mode=pl.Buffered(1))
    return pl.pallas_call(
        _sb_proj_kernel,
        grid=(t // tm,),
        in_specs=[pl.BlockSpec((tm, d), row), _const_spec((1, d)), once],
        out_specs=[pl.BlockSpec((tm, SB_W), row), pl.BlockSpec((tm, SB_W), row),
                   pl.BlockSpec((tm // TK, SB_W, TK), lambda i: (i, 0, 0))],
        out_shape=[jax.ShapeDtypeStruct((t, SB_W), BF16), jax.ShapeDtypeStruct((t, SB_W), BF16),
                   jax.ShapeDtypeStruct((t // TK, SB_W, TK), BF16)],
        scratch_shapes=[pltpu.VMEM((d, 2 * SB_W), BF16), pltpu.VMEM((SB_W, d), BF16)],
        compiler_params=_params(("arbitrary",), 56 * 1024 * 1024),
        name="sb_proj",
    )(x2, g.reshape(1, d), w_qkv)


SOFTPLUS_LINEAR_FROM = 64.0


def _sb_attn_kernel(q_ref, k_ref, vt_ref, o_ref, carry_scr, acc_scr, lb_scr, top_scr, sp_scr, a_scr):
    step = pl.program_id(1)
    heads, hd = SB_HEADS, SB_HEAD_DIM
    row = lax.broadcasted_iota(jnp.int32, (TK, TK), 0)
    col = lax.broadcasted_iota(jnp.int32, (TK, TK), 1)
    suffix = jnp.where((col > row) | (col == 0), 1.0, 0.0).astype(BF16)
    first_row = lax.broadcasted_iota(jnp.int32, (SUM_ROWS, TQ), 0) == 0
    carry_scr[...] = jnp.zeros(carry_scr.shape, F32)
    acc_scr[...] = jnp.zeros(acc_scr.shape, F32)

    def process(groups):
        units = [(r, kj, masked, h) for r, kj, masked in groups for h in range(heads)]
        key = lax.broadcasted_iota(jnp.int32, (TK, TQ), 0)
        qry = lax.broadcasted_iota(jnp.int32, (TK, TQ), 1)
        for u, (r, kj, masked, h) in enumerate(units):
            koff = _key_offset(kj)
            tile = slice(r * TQ, (r + 1) * TQ)
            cols = slice(h * hd, (h + 1) * hd)
            z = _dot_nt(k_ref[pl.ds(koff, TK), cols], q_ref[tile, cols])
            sp = jnp.where(z > SOFTPLUS_LINEAR_FROM, z, jnp.log2(1.0 + jnp.exp2(z)))
            lb_scr[u] = z - sp
            if masked:
                sp = jnp.where(key < qry, sp, 0.0)
            top_scr[u] = sp[:SUM_ROWS, :]
            sp_scr[u] = sp[SUM_ROWS:, :].astype(BF16)
        for u, (r, kj, masked, h) in enumerate(units):
            tile = slice(r * TQ, (r + 1) * TQ)
            seen = carry_scr[h:h + 1, tile]
            top = top_scr[u]
            rhs = jnp.concatenate(
                [jnp.where(first_row, seen, top).astype(BF16), sp_scr[u]], axis=0)
            later = _dot(suffix, rhs)
            seen_as_summed = seen.astype(BF16).astype(F32)
            carry_scr[h:h + 1, tile] = seen + top[0:1, :] + (later[0:1, :] - seen_as_summed)
            a = jnp.exp2(lb_scr[u] - later)
            if masked:
                a = jnp.where(key < qry, a, 0.0)
            a_scr[u] = a.astype(BF16)
        for u, (r, kj, masked, h) in enumerate(units):
            tile = slice(r * TQ, (r + 1) * TQ)
            cols = slice(h * hd, (h + 1) * hd)
            acc_scr[cols, tile] = acc_scr[cols, tile] + _dot(vt_ref[kj, cols, :], a_scr[u])

    process(_diag_groups(step, "right_to_left"))

    def body(i, carry):
        kj = step * Q_TILES - 1 - i * CHUNKS_PER_ITER
        process([(r, kj - c, False) for c in range(CHUNKS_PER_ITER) for r in range(Q_TILES)])
        return carry

    lax.fori_loop(0, step * Q_TILES // CHUNKS_PER_ITER, body, 0)

    for r in range(Q_TILES):
        tile = slice(r * TQ, (r + 1) * TQ)
        for h in range(heads):
            cols = slice(h * hd, (h + 1) * hd)
            o_ref[tile, cols] = acc_scr[cols, tile].T.astype(BF16)


def _sb_attn(q, k, vt, batch, seq):
    t = q.shape[0]
    steps = seq // TQ_STEP
    nch = seq // TK
    k3 = k.reshape(batch, seq, SB_W)
    vt4 = vt.reshape(batch, nch, SB_W, TK)
    qrow = lambda b, i: (b * steps + i, 0)
    n_units = SB_HEADS * GROUPS_PER_BLOCK
    return pl.pallas_call(
        _sb_attn_kernel,
        grid=(batch, steps),
        in_specs=[pl.BlockSpec((TQ_STEP, SB_W), qrow),
                  pl.BlockSpec((None, seq, SB_W), lambda b, i: (b, 0, 0)),
                  pl.BlockSpec((None, nch, SB_W, TK), lambda b, i: (b, 0, 0, 0))],
        out_specs=pl.BlockSpec((TQ_STEP, SB_W), qrow),
        out_shape=jax.ShapeDtypeStruct((t, SB_W), BF16),
        scratch_shapes=[pltpu.VMEM((SB_HEADS, TQ_STEP), F32), pltpu.VMEM((SB_W, TQ_STEP), F32),
                        pltpu.VMEM((n_units, TK, TQ), F32),
                        pltpu.VMEM((n_units, SUM_ROWS, TQ), F32),
                        pltpu.VMEM((n_units, TK - SUM_ROWS, TQ), BF16),
                        pltpu.VMEM((n_units, TK, TQ), BF16)],
        compiler_params=_params(("parallel", "arbitrary"), 48 * 1024 * 1024),
        name="sb_attn",
    )(q, k3, vt4)


MEM_SCALE = MEM_HEAD_DIM ** -0.5 * LOG2E
MEM_W = MEM_HEADS * MEM_HEAD_DIM


def _cross_kernel(h_ref, o_ref, wo_ref, g_ref, wq_ref, kv_ref, wxo_ref, win_ref, wout_ref,
                  out_ref, win_bf_ref, wout_bf_ref):
    win_bf_ref[...] = win_ref[...].astype(BF16)
    wout_bf_ref[...] = wout_ref[...].astype(BF16)

    h1 = h_ref[...] + _dot(o_ref[...], wo_ref[...])
    n = _rmsnorm(h1, g_ref[...]).astype(BF16)
    q = (_dot(n, wq_ref[...]) * MEM_SCALE).astype(BF16)
    head_cols = [slice(hh * MEM_HEAD_DIM, (hh + 1) * MEM_HEAD_DIM) for hh in range(MEM_HEADS)]
    scores = [_dot_nt(q[:, c], kv_ref[:, c]) for c in head_cols]
    probs, denoms = [], []
    for s in scores:
        p = jnp.exp2(s - jnp.max(s, axis=-1, keepdims=True))
        denoms.append(jnp.sum(p, axis=-1, keepdims=True))
        probs.append(p.astype(BF16))
    outs = []
    for c, p, denom in zip(head_cols, probs, denoms):
        v_h = kv_ref[:, MEM_W + c.start:MEM_W + c.stop]
        outs.append((_dot(p, v_h) / denom).astype(BF16))
    o = jnp.concatenate(outs, axis=1)
    out_ref[...] = h1 + _dot(o, wxo_ref[...])


def _cross(h2, o, w_o, w_o_layer, g, w_q, kv, w_xo, mlp_w_in, mlp_w_out, layer, seq):
    t, d = h2.shape
    tm = TM
    steps = t // tm
    per_batch = seq // tm
    row = lambda i: (i, 0)
    slab_in = mlp_w_in.shape[1] // steps
    slab_out = mlp_w_out.shape[1] // steps
    return pl.pallas_call(
        _cross_kernel,
        grid=(steps,),
        in_specs=[pl.BlockSpec((tm, d), row), pl.BlockSpec((tm, o.shape[1]), row),
                  _layer_spec(w_o, w_o_layer), _layer_spec(g, layer), _layer_spec(w_q, layer),
                  pl.BlockSpec((None, None) + kv.shape[2:],
                               lambda i: (layer, i // per_batch, 0, 0)),
                  _layer_spec(w_xo, layer),
                  pl.BlockSpec((None, slab_in, mlp_w_in.shape[2]), lambda i: (layer, i, 0)),
                  pl.BlockSpec((None, slab_out, mlp_w_out.shape[2]), lambda i: (layer, i, 0))],
        out_specs=[pl.BlockSpec((tm, d), row),
                   pl.BlockSpec((slab_in, mlp_w_in.shape[2]), row),
                   pl.BlockSpec((slab_out, mlp_w_out.shape[2]), row)],
        out_shape=[jax.ShapeDtypeStruct((t, d), F32),
                   jax.ShapeDtypeStruct(mlp_w_in.shape[1:], BF16),
                   jax.ShapeDtypeStruct(mlp_w_out.shape[1:], BF16)],
        compiler_params=_params(("parallel",), 56 * 1024 * 1024),
        name="cross_attn",
    )(h2, o, w_o, g, w_q, kv, w_xo, mlp_w_in, mlp_w_out)


def _mlp_kernel(h_ref, g_ref, win_ref, wout_ref, gf_ref, out_ref, *, final_norm):
    h = h_ref[...]
    n = _rmsnorm(h, g_ref[...]).astype(BF16)
    acc = h
    d_ff = win_ref.shape[1]
    for c in range(d_ff // FF_CHUNK):
        cols = slice(c * FF_CHUNK, (c + 1) * FF_CHUNK)
        u = jnp.maximum(_dot(n, win_ref[:, cols]), 0.0)
        acc = acc + _dot((u * u).astype(BF16), wout_ref[cols, :])
    if final_norm:
        acc = _rmsnorm(acc, gf_ref[...])
    out_ref[...] = acc


def _mlp(h2, g, w_in, w_out, layer, g_final, final_norm):
    t, d = h2.shape
    tm = TM
    row = lambda i: (i, 0)
    return pl.pallas_call(
        functools.partial(_mlp_kernel, final_norm=final_norm),
        grid=(t // tm,),
        in_specs=[pl.BlockSpec((tm, d), row), _layer_spec(g, layer),
                  _const_spec(w_in.shape), _const_spec(w_out.shape), _const_spec((1, d))],
        out_specs=pl.BlockSpec((tm, d), row),
        out_shape=jax.ShapeDtypeStruct((t, d), F32),
        compiler_params=_params(("parallel",), 56 * 1024 * 1024),
        name="mlp",
    )(h2, g, w_in, w_out, g_final.reshape(1, d))


def kernel(x, mem, positions, norm_mix, norm_cross, norm_mem, norm_mlp, norm_final,
           mla_w_dkv, mla_g_q, mla_g_kv, mla_w_uq, mla_w_ukv, mla_w_o,
           sb_w_qkv, sb_w_o, xa_w_q, xa_w_kv, xa_w_o, mlp_w_in, mlp_w_out):
    batch, seq, d = x.shape
    depth = norm_mix.shape[0]
    assert depth == 2 and seq % TM == 0 and TM % TK == 0 and TQ == TK and seq % TQ_STEP == 0
    assert Q_TILES % CHUNKS_PER_ITER == 0
    h = x.reshape(batch * seq, d)
    cos, sin = _rope_tables(positions)
    mem_kv = _mem_kv(mem, norm_mem, xa_w_kv.astype(BF16))

    mla_wo, sb_wo = mla_w_o.astype(BF16), sb_w_o.astype(BF16)
    xa_wq, xa_wo = xa_w_q.astype(BF16), xa_w_o.astype(BF16)
    g_cross, g_mlp = _rows(norm_cross), _rows(norm_mlp)

    for i in range(depth):
        j = i // 2
        if i % 2 == 0:
            qn, qp, kn, kpe, vt = _mla_proj(h, norm_mix[i], mla_w_dkv[j], mla_g_q[j],
                                            mla_g_kv[j], mla_w_uq[j], mla_w_ukv[j], cos, sin)
            o = _mla_attn(qn, qp, kn, kpe, vt, batch, seq)
            w_o = mla_wo
        else:
            q, k, vt = _sb_proj(h, norm_mix[i], sb_w_qkv[j])
            o = _sb_attn(q, k, vt, batch, seq)
            w_o = sb_wo
        h, w_in, w_out = _cross(h, o, w_o, j, g_cross, xa_wq, mem_kv, xa_wo,
                                mlp_w_in, mlp_w_out, i, seq)
        h = _mlp(h, g_mlp, w_in, w_out, i, norm_final, final_norm=(i == depth - 1))
    return h.reshape(batch, seq, d)
```

```python
import functools
import math

import jax
import jax.numpy as jnp
from jax import lax
from jax.experimental import pallas as pl
from jax.experimental.pallas import tpu as pltpu

F32 = jnp.float32
BF16 = jnp.bfloat16

D_MODEL = 1024
EPS = 1e-6
MLA_HEADS = 8
MLA_Q_LORA = 384
MLA_KV_LORA = 256
MLA_NOPE = 128
MLA_ROPE = 64
MLA_V = 128
ROPE_THETA = 10000.0
SB_HEADS = 8
SB_HEAD_DIM = 128
MEM_HEADS = 4
MEM_HEAD_DIM = 128
NEG_BIG = -1e30
LOG2E = math.log2(math.e)

LANES = 128
V7X_VMEM_BYTES = 64 * 1024 * 1024

TQ = 256
TK = 256
TM = 1024
FF_CHUNK = 1024


def _vmem_limit(nbytes):
    return int(min(max(nbytes, 32 * 1024 * 1024), V7X_VMEM_BYTES - 8 * 1024 * 1024))


def _params(semantics, vmem_bytes):
    return pltpu.CompilerParams(dimension_semantics=semantics,
                                vmem_limit_bytes=_vmem_limit(vmem_bytes))


def _const_spec(shape):
    nd = len(shape)
    return pl.BlockSpec(shape, lambda *_: (0,) * nd)


def _layer_spec(stacked, layer):
    rest = stacked.shape[1:]
    return pl.BlockSpec((None,) + rest, lambda *_: (layer,) + (0,) * len(rest))


def _rows(param):
    return param.reshape(param.shape[0], 1, param.shape[1])


def _cast_jobs(stacked_weights, layers, n_steps, slab_of):
    in_specs, out_specs, out_shapes = [], [], []
    for w, layer in zip(stacked_weights, layers):
        rows, cols = w.shape[1] // n_steps, w.shape[2]
        assert rows * n_steps == w.shape[1] and rows % 16 == 0
        in_specs.append(pl.BlockSpec((None, rows, cols),
                                     lambda *g, layer=layer: (layer, slab_of(*g), 0)))
        out_specs.append(pl.BlockSpec((rows, cols), lambda *g: (slab_of(*g), 0)))
        out_shapes.append(jax.ShapeDtypeStruct(w.shape[1:], BF16))
    return in_specs, out_specs, out_shapes


def _run_cast_jobs(src_refs, dst_refs):
    for src, dst in zip(src_refs, dst_refs):
        dst[...] = src[...].astype(BF16)


def _rmsnorm(x, g):
    return x * lax.rsqrt(jnp.mean(x * x, axis=-1, keepdims=True) + EPS) * g


def _dot(a, b):
    return jnp.dot(a, b, preferred_element_type=F32)


def _dot_nt(a, b):
    return lax.dot_general(a, b, (((1,), (1,)), ((), ())), preferred_element_type=F32)


ROPE_HALF = MLA_ROPE // 2
ROPE_PER_ROW = LANES // ROPE_HALF


def _exact_bf16_terms(x):
    hi = x.astype(BF16)
    rest = x - hi.astype(F32)
    mid = rest.astype(BF16)
    lo = (rest - mid.astype(F32)).astype(BF16)
    return hi, mid, lo


def _rope_kernel(pos_ref, freq_ref, cos_ref, sin_ref):
    ang = pos_ref[...].astype(F32) * freq_ref[...]
    src = lax.broadcasted_iota(jnp.int32, (LANES, LANES), 0)
    dst = lax.broadcasted_iota(jnp.int32, (LANES, LANES), 1)
    for table, out_ref in ((jnp.cos(ang), cos_ref), (jnp.sin(ang), sin_ref)):
        terms = _exact_bf16_terms(table)
        for j in range(ROPE_PER_ROW):
            spread = jnp.where(src == j * ROPE_HALF + (dst & (ROPE_HALF - 1)), 1.0, 0.0).astype(BF16)
            out_ref[j] = _dot(terms[0], spread) + _dot(terms[1], spread) + _dot(terms[2], spread)


def _rope_tables(positions):
    n_tok = positions.size
    inv_freq = ROPE_THETA ** (-jnp.arange(0, MLA_ROPE, 2, dtype=F32) / MLA_ROPE)
    rows = n_tok // ROPE_PER_ROW
    pos_rep = jnp.repeat(positions.reshape(ROPE_PER_ROW, rows).T, ROPE_HALF, axis=1)
    freq = jnp.tile(inv_freq, ROPE_PER_ROW).reshape(1, LANES)
    rb = 512
    out_spec = pl.BlockSpec((ROPE_PER_ROW, rb, LANES), lambda i: (0, i, 0))
    cos, sin = pl.pallas_call(
        _rope_kernel,
        grid=(rows // rb,),
        in_specs=[pl.BlockSpec((rb, LANES), lambda i: (i, 0)), _const_spec((1, LANES))],
        out_specs=[out_spec] * 2,
        out_shape=[jax.ShapeDtypeStruct((ROPE_PER_ROW, rows, LANES), F32)] * 2,
        compiler_params=_params(("parallel",), 0),
        name="rope_tables",
    )(pos_rep, freq)
    return cos.reshape(n_tok, LANES), sin.reshape(n_tok, LANES)


def _mem_kv_kernel(mem_ref, g_ref, w_ref, kv_ref):
    n = _rmsnorm(mem_ref[...], g_ref[...]).astype(BF16)
    kv_ref[...] = _dot(n, w_ref[...]).astype(BF16)


def _mem_kv(mem, norm_mem, w_kv):
    depth = norm_mem.shape[0]
    b, m, d = mem.shape
    n_out = w_kv.shape[-1]
    return pl.pallas_call(
        _mem_kv_kernel,
        grid=(depth, b),
        in_specs=[pl.BlockSpec((None, m, d), lambda l, i: (i, 0, 0)),
                  pl.BlockSpec((None, 1, d), lambda l, i: (l, 0, 0)),
                  pl.BlockSpec((None, d, n_out), lambda l, i: (l, 0, 0))],
        out_specs=pl.BlockSpec((None, None, m, n_out), lambda l, i: (l, i, 0, 0)),
        out_shape=jax.ShapeDtypeStruct((depth, b, m, n_out), BF16),
        compiler_params=_params(("parallel", "parallel"), 0),
        name="mem_kv",
    )(mem, norm_mem.reshape(depth, 1, d), w_kv)


MLA_QK_SCALE = (MLA_NOPE + MLA_ROPE) ** -0.5 * LOG2E
LAT_Q_END = MLA_Q_LORA
LAT_KV_END = MLA_Q_LORA + MLA_KV_LORA
LAT_EXT = LAT_KV_END + 2 * MLA_ROPE
Q_NOPE_W = MLA_HEADS * MLA_NOPE
Q_ROPE_W = MLA_HEADS * MLA_ROPE


def _mla_proj_kernel(x_ref, g_ref, wdkv_ref, gq_ref, gkv_ref, wuq_ref, wuk_ref, wvt_ref,
                     cos_ref, sin_ref, qn_ref, qp_ref, kn_ref, kpe_ref, vt_ref):
    a = _rmsnorm(x_ref[...], g_ref[...]).astype(BF16)
    lat = _dot(a, wdkv_ref[...])
    c_q = _rmsnorm(lat[:, :LAT_Q_END], gq_ref[...]).astype(BF16)
    c_kv = _rmsnorm(lat[:, LAT_Q_END:LAT_KV_END], gkv_ref[...]).astype(BF16)
    cos = cos_ref[...]
    sin = sin_ref[...]

    slab = lat[:, LAT_KV_END:LAT_EXT]
    rot = slab * cos + pltpu.roll(slab, MLA_ROPE, 1) * sin
    lane = lax.broadcasted_iota(jnp.int32, rot.shape, 1)
    k_even = jnp.where(lane < MLA_ROPE, rot, 0.0)
    k_odd = pltpu.roll(k_even, MLA_ROPE, 1)
    kpe_ref[...] = jnp.concatenate([k_even, k_odd], axis=1).astype(BF16)

    q = _dot(c_q, wuq_ref[...])
    qn_ref[...] = (q[:, :Q_NOPE_W] * MLA_QK_SCALE).astype(BF16)
    for s in range(Q_ROPE_W // LANES):
        lo = Q_NOPE_W + s * LANES
        r = q[:, lo:lo + LANES]
        r_partner = q[:, lo + Q_ROPE_W:lo + Q_ROPE_W + LANES]
        qp_ref[:, s * LANES:(s + 1) * LANES] = (
            (r * cos + r_partner * sin) * MLA_QK_SCALE).astype(BF16)

    kn_ref[...] = _dot(c_kv, wuk_ref[...]).astype(BF16)
    vt = _dot_nt(wvt_ref[...], c_kv)
    for c in range(vt_ref.shape[0]):
        vt_ref[c] = vt[:, c * TK:(c + 1) * TK].astype(BF16)


def _mla_proj(x2, g, w_dkv, g_q, g_kv, w_uq, w_ukv, cos, sin):
    t, d = x2.shape
    h = MLA_HEADS
    kpe0 = LAT_KV_END
    half = MLA_ROPE // 2
    wdkv = jnp.concatenate(
        [w_dkv, -w_dkv[:, kpe0 + half:kpe0 + MLA_ROPE], w_dkv[:, kpe0:kpe0 + half]],
        axis=1).astype(BF16)
    uq = w_uq.reshape(MLA_Q_LORA, h, MLA_NOPE + MLA_ROPE)
    uq_rope = uq[:, :, MLA_NOPE:]
    uq_partner = jnp.concatenate([-uq_rope[:, :, half:], uq_rope[:, :, :half]], axis=-1)
    wuq = jnp.concatenate(
        [uq[:, :, :MLA_NOPE].reshape(MLA_Q_LORA, Q_NOPE_W),
         uq_rope.reshape(MLA_Q_LORA, Q_ROPE_W),
         uq_partner.reshape(MLA_Q_LORA, Q_ROPE_W)], axis=1).astype(BF16)
    ukv = w_ukv.reshape(MLA_KV_LORA, h, MLA_NOPE + MLA_V)
    wuk = ukv[:, :, :MLA_NOPE].reshape(MLA_KV_LORA, h * MLA_NOPE).astype(BF16)
    wvt = ukv[:, :, MLA_NOPE:].reshape(MLA_KV_LORA, h * MLA_V).T.astype(BF16)

    tm = TM
    n_chunks = t // TK
    row = lambda i: (i, 0)
    out_shapes = [
        jax.ShapeDtypeStruct((t, Q_NOPE_W), BF16),
        jax.ShapeDtypeStruct((t, Q_ROPE_W), BF16),
        jax.ShapeDtypeStruct((t, h * MLA_NOPE), BF16),
        jax.ShapeDtypeStruct((t, 2 * LANES), BF16),
        jax.ShapeDtypeStruct((n_chunks, h * MLA_V, TK), BF16),
    ]
    return pl.pallas_call(
        _mla_proj_kernel,
        grid=(t // tm,),
        in_specs=[pl.BlockSpec((tm, d), row), _const_spec((1, d)),
                  _const_spec(wdkv.shape), _const_spec((1, MLA_Q_LORA)),
                  _const_spec((1, MLA_KV_LORA)), _const_spec(wuq.shape),
                  _const_spec(wuk.shape), _const_spec(wvt.shape),
                  pl.BlockSpec((tm, LANES), row), pl.BlockSpec((tm, LANES), row)],
        out_specs=[pl.BlockSpec((tm, Q_NOPE_W), row), pl.BlockSpec((tm, Q_ROPE_W), row),
                   pl.BlockSpec((tm, h * MLA_NOPE), row), pl.BlockSpec((tm, 2 * LANES), row),
                   pl.BlockSpec((tm // TK, h * MLA_V, TK), lambda i: (i, 0, 0))],
        out_shape=out_shapes,
        compiler_params=_params(("parallel",), 48 * 1024 * 1024),
        name="mla_proj",
    )(x2, g.reshape(1, d), wdkv, g_q.reshape(1, -1), g_kv.reshape(1, -1), wuq, wuk, wvt,
      cos, sin)


Q_TILES = 2
SUM_ROWS = 16
TQ_STEP = Q_TILES * TQ
CHUNKS_PER_ITER = 2
GROUPS_PER_BLOCK = max(Q_TILES * (Q_TILES + 1) // 2, Q_TILES * CHUNKS_PER_ITER)


def _diag_groups(step, mixer_order):
    groups = []
    for r in range(Q_TILES):
        own = [(r, step * Q_TILES + c, c == r) for c in range(r + 1)]
        groups += own if mixer_order == "any" else own[::-1]
    return groups


def _key_offset(kj):
    off = kj * TK
    return off if isinstance(off, int) else pl.multiple_of(off, TK)


def _mla_attn_kernel(qn_ref, qp_ref, kn_ref, kpe_ref, vt_ref, *refs, n_jobs):
    o_ref = refs[n_jobs]
    _run_cast_jobs(refs[:n_jobs], refs[n_jobs + 1:2 * n_jobs + 1])
    m_scr, l_scr, acc_scr, s_scr = refs[2 * n_jobs + 1:]
    step = pl.program_id(1)
    heads, hd = MLA_HEADS, MLA_NOPE
    m_scr[...] = jnp.full(m_scr.shape, NEG_BIG, F32)
    l_scr[...] = jnp.zeros(l_scr.shape, F32)
    acc_scr[...] = jnp.zeros(acc_scr.shape, F32)

    def process(groups):
        units = [(r, kj, masked, h) for r, kj, masked in groups for h in range(heads)]
        key = lax.broadcasted_iota(jnp.int32, (TK, TQ), 0)
        qry = lax.broadcasted_iota(jnp.int32, (TK, TQ), 1)
        stats = []
        for u, (r, kj, masked, h) in enumerate(units):
            koff = _key_offset(kj)
            tile = slice(r * TQ, (r + 1) * TQ)
            cols = slice(h * hd, (h + 1) * hd)
            pair = slice((h // 2) * LANES, (h // 2 + 1) * LANES)
            par = slice((h % 2) * LANES, (h % 2 + 1) * LANES)
            q_h = jnp.concatenate([qn_ref[tile, cols], qp_ref[tile, pair]], axis=1)
            k_h = jnp.concatenate([kn_ref[pl.ds(koff, TK), cols],
                                   kpe_ref[pl.ds(koff, TK), par]], axis=1)
            s = _dot_nt(k_h, q_h)
            if masked:
                s = jnp.where(key <= qry, s, NEG_BIG)
            s_scr[u] = s
            m_prev = m_scr[h:h + 1, tile]
            m_new = jnp.maximum(m_prev, jnp.max(s, axis=0, keepdims=True))
            m_scr[h:h + 1, tile] = m_new
            stats.append((m_new, jnp.exp2(m_prev - m_new)))
        p_bf = [jnp.exp2(s_scr[u] - stats[u][0]).astype(BF16) for u in range(len(units))]
        ones = jnp.ones((SUM_ROWS, TK), BF16)
        for u, (r, kj, masked, h) in enumerate(units):
            tile = slice(r * TQ, (r + 1) * TQ)
            cols = slice(h * hd, (h + 1) * hd)
            alpha = stats[u][1]
            v_ext = jnp.concatenate([vt_ref[kj, cols, :], ones], axis=0)
            pv = _dot(v_ext, p_bf[u])
            acc_scr[cols, tile] = alpha * acc_scr[cols, tile] + pv[:hd, :]
            l_scr[h:h + 1, tile] = alpha * l_scr[h:h + 1, tile] + pv[hd:hd + 1, :]

    process(_diag_groups(step, "any"))

    def body(it, carry):
        process([(r, it * CHUNKS_PER_ITER + c, False)
                 for c in range(CHUNKS_PER_ITER) for r in range(Q_TILES)])
        return carry

    lax.fori_loop(0, step * Q_TILES // CHUNKS_PER_ITER, body, 0)

    for r in range(Q_TILES):
        tile = slice(r * TQ, (r + 1) * TQ)
        for h in range(heads):
            cols = slice(h * hd, (h + 1) * hd)
            o_t = acc_scr[cols, tile] / l_scr[h:h + 1, tile]
            o_ref[tile, cols] = o_t.T.astype(BF16)


def _mla_attn(qn, qp, kn, kpe, vt, batch, seq, cast_weights, cast_layers):
    t = qn.shape[0]
    steps = seq // TQ_STEP
    job_in, job_out, job_shapes = _cast_jobs(cast_weights, cast_layers, batch * steps,
                                             lambda b, i: b * steps + i)
    nch = seq // TK
    hv = MLA_HEADS * MLA_V
    kn3 = kn.reshape(batch, seq, -1)
    kpe3 = kpe.reshape(batch, seq, -1)
    vt4 = vt.reshape(batch, nch, hv, TK)
    qrow = lambda b, i: (b * steps + i, 0)
    perb = lambda b, i: (b, 0, 0)
    n_units = MLA_HEADS * GROUPS_PER_BLOCK
    return pl.pallas_call(
        functools.partial(_mla_attn_kernel, n_jobs=len(job_in)),
        grid=(batch, steps),
        in_specs=[pl.BlockSpec((TQ_STEP, qn.shape[1]), qrow),
                  pl.BlockSpec((TQ_STEP, qp.shape[1]), qrow),
                  pl.BlockSpec((None, seq, kn3.shape[2]), perb),
                  pl.BlockSpec((None, seq, kpe3.shape[2]), perb),
                  pl.BlockSpec((None, nch, hv, TK), lambda b, i: (b, 0, 0, 0))] + job_in,
        out_specs=[pl.BlockSpec((TQ_STEP, hv), qrow)] + job_out,
        out_shape=[jax.ShapeDtypeStruct((t, hv), BF16)] + job_shapes,
        scratch_shapes=[pltpu.VMEM((MLA_HEADS, TQ_STEP), F32), pltpu.VMEM((MLA_HEADS, TQ_STEP), F32),
                        pltpu.VMEM((hv, TQ_STEP), F32), pltpu.VMEM((n_units, TK, TQ), F32)],
        compiler_params=_params(("parallel", "arbitrary"), 48 * 1024 * 1024),
        name="mla_attn",
    )(qn, qp, kn3, kpe3, vt4, *cast_weights)


SB_SCALE = SB_HEAD_DIM ** -0.5 * LOG2E
SB_W = SB_HEADS * SB_HEAD_DIM


def _sb_proj_kernel(x_ref, g_ref, w_ref, q_ref, k_ref, vt_ref, wqk_scr, wvt_scr):
    @pl.when(pl.program_id(0) == 0)
    def _():
        wqk_scr[...] = w_ref[:, :2 * SB_W].astype(BF16)
        wvt_scr[...] = w_ref[:, 2 * SB_W:].T.astype(BF16)

    a = _rmsnorm(x_ref[...], g_ref[...]).astype(BF16)
    qk = _dot(a, wqk_scr[...])
    q_ref[...] = (qk[:, :SB_W] * SB_SCALE).astype(BF16)
    k_ref[...] = qk[:, SB_W:].astype(BF16)
    vt = _dot_nt(wvt_scr[...], a)
    for c in range(vt_ref.shape[0]):
        vt_ref[c] = vt[:, c * TK:(c + 1) * TK].astype(BF16)


def _sb_proj(x2, g, w_qkv):
    t, d = x2.shape
    tm = TM
    row = lambda i: (i, 0)
    once = pl.BlockSpec(w_qkv.shape, lambda i: (0, 0), pipeline_mode=pl.Buffered(1))
    return pl.pallas_call(
        _sb_proj_kernel,
        grid=(t // tm,),
        in_specs=[pl.BlockSpec((tm, d), row), _const_spec((1, d)), once],
        out_specs=[pl.BlockSpec((tm, SB_W), row), pl.BlockSpec((tm, SB_W), row),
                   pl.BlockSpec((tm // TK, SB_W, TK), lambda i: (i, 0, 0))],
        out_shape=[jax.ShapeDtypeStruct((t, SB_W), BF16), jax.ShapeDtypeStruct((t, SB_W), BF16),
                   jax.ShapeDtypeStruct((t // TK, SB_W, TK), BF16)],
        scratch_shapes=[pltpu.VMEM((d, 2 * SB_W), BF16), pltpu.VMEM((SB_W, d), BF16)],
        compiler_params=_params(("arbitrary",), 56 * 1024 * 1024),
        name="sb_proj",
    )(x2, g.reshape(1, d), w_qkv)


SOFTPLUS_LINEAR_FROM = 64.0


def _sb_attn_kernel(q_ref, k_ref, vt_ref, *refs, n_jobs):
    o_ref = refs[n_jobs]
    _run_cast_jobs(refs[:n_jobs], refs[n_jobs + 1:2 * n_jobs + 1])
    carry_scr, acc_scr, lb_scr, top_scr, sp_scr, a_scr = refs[2 * n_jobs + 1:]
    step = pl.program_id(1)
    heads, hd = SB_HEADS, SB_HEAD_DIM
    row = lax.broadcasted_iota(jnp.int32, (TK, TK), 0)
    col = lax.broadcasted_iota(jnp.int32, (TK, TK), 1)
    suffix = jnp.where((col > row) | (col == 0), 1.0, 0.0).astype(BF16)
    first_row = lax.broadcasted_iota(jnp.int32, (SUM_ROWS, TQ), 0) == 0
    carry_scr[...] = jnp.zeros(carry_scr.shape, F32)
    acc_scr[...] = jnp.zeros(acc_scr.shape, F32)

    def process(groups):
        units = [(r, kj, masked, h) for r, kj, masked in groups for h in range(heads)]
        key = lax.broadcasted_iota(jnp.int32, (TK, TQ), 0)
        qry = lax.broadcasted_iota(jnp.int32, (TK, TQ), 1)
        for u, (r, kj, masked, h) in enumerate(units):
            koff = _key_offset(kj)
            tile = slice(r * TQ, (r + 1) * TQ)
            cols = slice(h * hd, (h + 1) * hd)
            z = _dot_nt(k_ref[pl.ds(koff, TK), cols], q_ref[tile, cols])
            sp = jnp.where(z > SOFTPLUS_LINEAR_FROM, z, jnp.log2(1.0 + jnp.exp2(z)))
            lb_scr[u] = z - sp
            if masked:
                sp = jnp.where(key < qry, sp, 0.0)
            top_scr[u] = sp[:SUM_ROWS, :]
            sp_scr[u] = sp[SUM_ROWS:, :].astype(BF16)
        for u, (r, kj, masked, h) in enumerate(units):
            tile = slice(r * TQ, (r + 1) * TQ)
            seen = carry_scr[h:h + 1, tile]
            top = top_scr[u]
            rhs = jnp.concatenate(
                [jnp.where(first_row, seen, top).astype(BF16), sp_scr[u]], axis=0)
            later = _dot(suffix, rhs)
            seen_as_summed = seen.astype(BF16).astype(F32)
            carry_scr[h:h + 1, tile] = seen + top[0:1, :] + (later[0:1, :] - seen_as_summed)
            a = jnp.exp2(lb_scr[u] - later)
            if masked:
                a = jnp.where(key < qry, a, 0.0)
            a_scr[u] = a.astype(BF16)
        for u, (r, kj, masked, h) in enumerate(units):
            tile = slice(r * TQ, (r + 1) * TQ)
            cols = slice(h * hd, (h + 1) * hd)
            acc_scr[cols, tile] = acc_scr[cols, tile] + _dot(vt_ref[kj, cols, :], a_scr[u])

    process(_diag_groups(step, "right_to_left"))

    def body(i, carry):
        kj = step * Q_TILES - 1 - i * CHUNKS_PER_ITER
        process([(r, kj - c, False) for c in range(CHUNKS_PER_ITER) for r in range(Q_TILES)])
        return carry

    lax.fori_loop(0, step * Q_TILES // CHUNKS_PER_ITER, body, 0)

    for r in range(Q_TILES):
        tile = slice(r * TQ, (r + 1) * TQ)
        for h in range(heads):
            cols = slice(h * hd, (h + 1) * hd)
            o_ref[tile, cols] = acc_scr[cols, tile].T.astype(BF16)


def _sb_attn(q, k, vt, batch, seq, cast_weights, cast_layers):
    t = q.shape[0]
    steps = seq // TQ_STEP
    job_in, job_out, job_shapes = _cast_jobs(cast_weights, cast_layers, batch * steps,
                                             lambda b, i: b * steps + i)
    nch = seq // TK
    k3 = k.reshape(batch, seq, SB_W)
    vt4 = vt.reshape(batch, nch, SB_W, TK)
    qrow = lambda b, i: (b * steps + i, 0)
    n_units = SB_HEADS * GROUPS_PER_BLOCK
    return pl.pallas_call(
        functools.partial(_sb_attn_kernel, n_jobs=len(job_in)),
        grid=(batch, steps),
        in_specs=[pl.BlockSpec((TQ_STEP, SB_W), qrow),
                  pl.BlockSpec((None, seq, SB_W), lambda b, i: (b, 0, 0)),
                  pl.BlockSpec((None, nch, SB_W, TK), lambda b, i: (b, 0, 0, 0))] + job_in,
        out_specs=[pl.BlockSpec((TQ_STEP, SB_W), qrow)] + job_out,
        out_shape=[jax.ShapeDtypeStruct((t, SB_W), BF16)] + job_shapes,
        scratch_shapes=[pltpu.VMEM((SB_HEADS, TQ_STEP), F32), pltpu.VMEM((SB_W, TQ_STEP), F32),
                        pltpu.VMEM((n_units, TK, TQ), F32),
                        pltpu.VMEM((n_units, SUM_ROWS, TQ), F32),
                        pltpu.VMEM((n_units, TK - SUM_ROWS, TQ), BF16),
                        pltpu.VMEM((n_units, TK, TQ), BF16)],
        compiler_params=_params(("parallel", "arbitrary"), 48 * 1024 * 1024),
        name="sb_attn",
    )(q, k3, vt4, *cast_weights)


MEM_SCALE = MEM_HEAD_DIM ** -0.5 * LOG2E
MEM_W = MEM_HEADS * MEM_HEAD_DIM


def _cross_kernel(h_ref, o_ref, wo_ref, g_ref, wq_ref, kv_ref, wxo_ref, out_ref):
    h1 = h_ref[...] + _dot(o_ref[...], wo_ref[...])
    n = _rmsnorm(h1, g_ref[...]).astype(BF16)
    q = (_dot(n, wq_ref[...]) * MEM_SCALE).astype(BF16)
    head_cols = [slice(hh * MEM_HEAD_DIM, (hh + 1) * MEM_HEAD_DIM) for hh in range(MEM_HEADS)]
    scores = [_dot_nt(q[:, c], kv_ref[:, c]) for c in head_cols]
    probs, denoms = [], []
    for s in scores:
        p = jnp.exp2(s - jnp.max(s, axis=-1, keepdims=True))
        denoms.append(jnp.sum(p, axis=-1, keepdims=True))
        probs.append(p.astype(BF16))
    outs = []
    for c, p, denom in zip(head_cols, probs, denoms):
        v_h = kv_ref[:, MEM_W + c.start:MEM_W + c.stop]
        outs.append((_dot(p, v_h) / denom).astype(BF16))
    o = jnp.concatenate(outs, axis=1)
    out_ref[...] = h1 + _dot(o, wxo_ref[...])


def _cross(h2, o, w_o, g, w_q, kv, w_xo, layer, seq):
    t, d = h2.shape
    tm = TM
    per_batch = seq // tm
    row = lambda i: (i, 0)
    return pl.pallas_call(
        _cross_kernel,
        grid=(t // tm,),
        in_specs=[pl.BlockSpec((tm, d), row), pl.BlockSpec((tm, o.shape[1]), row),
                  _const_spec(w_o.shape), _layer_spec(g, layer), _const_spec(w_q.shape),
                  pl.BlockSpec((None, None) + kv.shape[2:],
                               lambda i: (layer, i // per_batch, 0, 0)),
                  _const_spec(w_xo.shape)],
        out_specs=pl.BlockSpec((tm, d), row),
        out_shape=jax.ShapeDtypeStruct((t, d), F32),
        compiler_params=_params(("parallel",), 48 * 1024 * 1024),
        name="cross_attn",
    )(h2, o, w_o, g, w_q, kv, w_xo)


def _mlp_kernel(h_ref, g_ref, win_ref, wout_ref, gf_ref, out_ref, *, final_norm):
    h = h_ref[...]
    n = _rmsnorm(h, g_ref[...]).astype(BF16)
    acc = h
    d_ff = win_ref.shape[1]
    for c in range(d_ff // FF_CHUNK):
        cols = slice(c * FF_CHUNK, (c + 1) * FF_CHUNK)
        u = jnp.maximum(_dot(n, win_ref[:, cols]), 0.0)
        acc = acc + _dot((u * u).astype(BF16), wout_ref[cols, :])
    if final_norm:
        acc = _rmsnorm(acc, gf_ref[...])
    out_ref[...] = acc


def _mlp(h2, g, w_in, w_out, layer, g_final, final_norm):
    t, d = h2.shape
    tm = TM
    row = lambda i: (i, 0)
    return pl.pallas_call(
        functools.partial(_mlp_kernel, final_norm=final_norm),
        grid=(t // tm,),
        in_specs=[pl.BlockSpec((tm, d), row), _layer_spec(g, layer),
                  _const_spec(w_in.shape), _const_spec(w_out.shape), _const_spec((1, d))],
        out_specs=pl.BlockSpec((tm, d), row),
        out_shape=jax.ShapeDtypeStruct((t, d), F32),
        compiler_params=_params(("parallel",), 56 * 1024 * 1024),
        name="mlp",
    )(h2, g, w_in, w_out, g_final.reshape(1, d))


def kernel(x, mem, positions, norm_mix, norm_cross, norm_mem, norm_mlp, norm_final,
           mla_w_dkv, mla_g_q, mla_g_kv, mla_w_uq, mla_w_ukv, mla_w_o,
           sb_w_qkv, sb_w_o, xa_w_q, xa_w_kv, xa_w_o, mlp_w_in, mlp_w_out):
    batch, seq, d = x.shape
    depth = norm_mix.shape[0]
    assert depth == 2 and seq % TM == 0 and TM % TK == 0 and TQ == TK and seq % TQ_STEP == 0
    assert Q_TILES % CHUNKS_PER_ITER == 0
    h = x.reshape(batch * seq, d)
    cos, sin = _rope_tables(positions)
    mem_kv = _mem_kv(mem, norm_mem, xa_w_kv.astype(BF16))

    g_cross, g_mlp = _rows(norm_cross), _rows(norm_mlp)

    for i in range(depth):
        j = i // 2
        if i % 2 == 0:
            casts = ((mla_w_o, xa_w_q, xa_w_o, mlp_w_in, mlp_w_out), (j, i, i, i, i))
            qn, qp, kn, kpe, vt = _mla_proj(h, norm_mix[i], mla_w_dkv[j], mla_g_q[j],
                                            mla_g_kv[j], mla_w_uq[j], mla_w_ukv[j], cos, sin)
            o, w_o, w_q, w_xo, w_in, w_out = _mla_attn(qn, qp, kn, kpe, vt, batch, seq, *casts)
        else:
            casts = ((sb_w_o, xa_w_q, xa_w_o, mlp_w_in, mlp_w_out), (j, i, i, i, i))
            q, k, vt = _sb_proj(h, norm_mix[i], sb_w_qkv[j])
            o, w_o, w_q, w_xo, w_in, w_out = _sb_attn(q, k, vt, batch, seq, *casts)
        h = _cross(h, o, w_o, g_cross, w_q, mem_kv, w_xo, i, seq)
        h = _mlp(h, g_mlp, w_in, w_out, i, norm_final, final_norm=(i == depth - 1))
    return h.reshape(batch, seq, d)
```

```python
import functools
import math

import jax
import jax.numpy as jnp
from jax import lax
from jax.experimental import pallas as pl
from jax.experimental.pallas import tpu as pltpu

F32 = jnp.float32
BF16 = jnp.bfloat16

D_MODEL = 1024
EPS = 1e-6
MLA_HEADS = 8
MLA_Q_LORA = 384
MLA_KV_LORA = 256
MLA_NOPE = 128
MLA_ROPE = 64
MLA_V = 128
ROPE_THETA = 10000.0
SB_HEADS = 8
SB_HEAD_DIM = 128
MEM_HEADS = 4
MEM_HEAD_DIM = 128
NEG_BIG = -1e30
LOG2E = math.log2(math.e)

LANES = 128
V7X_VMEM_BYTES = 64 * 1024 * 1024

TQ = 256
TK = 256
TM = 1024
FF_CHUNK = 1024


def _vmem_limit(nbytes):
    return int(min(max(nbytes, 32 * 1024 * 1024), V7X_VMEM_BYTES - 8 * 1024 * 1024))


def _params(semantics, vmem_bytes):
    return pltpu.CompilerParams(dimension_semantics=semantics,
                                vmem_limit_bytes=_vmem_limit(vmem_bytes))


def _const_spec(shape):
    nd = len(shape)
    return pl.BlockSpec(shape, lambda *_: (0,) * nd)


def _layer_spec(stacked, layer):
    rest = stacked.shape[1:]
    return pl.BlockSpec((None,) + rest, lambda *_: (layer,) + (0,) * len(rest))


def _rows(param):
    return param.reshape(param.shape[0], 1, param.shape[1])


def _cast_jobs(stacked_weights, layers, n_steps, slab_of):
    in_specs, out_specs, out_shapes = [], [], []
    for w, layer in zip(stacked_weights, layers):
        rows, cols = w.shape[1] // n_steps, w.shape[2]
        assert rows * n_steps == w.shape[1] and rows % 16 == 0
        in_specs.append(pl.BlockSpec((None, rows, cols),
                                     lambda *g, layer=layer: (layer, slab_of(*g), 0)))
        out_specs.append(pl.BlockSpec((rows, cols), lambda *g: (slab_of(*g), 0)))
        out_shapes.append(jax.ShapeDtypeStruct(w.shape[1:], BF16))
    return in_specs, out_specs, out_shapes


def _run_cast_jobs(src_refs, dst_refs):
    for src, dst in zip(src_refs, dst_refs):
        dst[...] = src[...].astype(BF16)


def _rmsnorm(x, g):
    return x * lax.rsqrt(jnp.mean(x * x, axis=-1, keepdims=True) + EPS) * g


def _dot(a, b):
    return jnp.dot(a, b, preferred_element_type=F32)


def _dot_nt(a, b):
    return lax.dot_general(a, b, (((1,), (1,)), ((), ())), preferred_element_type=F32)


ROPE_HALF = MLA_ROPE // 2
ROPE_PER_ROW = LANES // ROPE_HALF


def _exact_bf16_terms(x):
    hi = x.astype(BF16)
    rest = x - hi.astype(F32)
    mid = rest.astype(BF16)
    lo = (rest - mid.astype(F32)).astype(BF16)
    return hi, mid, lo


def _rope_kernel(pos_ref, freq_ref, cos_ref, sin_ref):
    ang = pos_ref[...].astype(F32) * freq_ref[...]
    src = lax.broadcasted_iota(jnp.int32, (LANES, LANES), 0)
    dst = lax.broadcasted_iota(jnp.int32, (LANES, LANES), 1)
    for table, out_ref in ((jnp.cos(ang), cos_ref), (jnp.sin(ang), sin_ref)):
        terms = _exact_bf16_terms(table)
        for j in range(ROPE_PER_ROW):
            spread = jnp.where(src == j * ROPE_HALF + (dst & (ROPE_HALF - 1)), 1.0, 0.0).astype(BF16)
            out_ref[j] = _dot(terms[0], spread) + _dot(terms[1], spread) + _dot(terms[2], spread)


def _rope_tables(positions):
    n_tok = positions.size
    inv_freq = ROPE_THETA ** (-jnp.arange(0, MLA_ROPE, 2, dtype=F32) / MLA_ROPE)
    rows = n_tok // ROPE_PER_ROW
    pos_rep = jnp.repeat(positions.reshape(ROPE_PER_ROW, rows).T, ROPE_HALF, axis=1)
    freq = jnp.tile(inv_freq, ROPE_PER_ROW).reshape(1, LANES)
    rb = 512
    out_spec = pl.BlockSpec((ROPE_PER_ROW, rb, LANES), lambda i: (0, i, 0))
    cos, sin = pl.pallas_call(
        _rope_kernel,
        grid=(rows // rb,),
        in_specs=[pl.BlockSpec((rb, LANES), lambda i: (i, 0)), _const_spec((1, LANES))],
        out_specs=[out_spec] * 2,
        out_shape=[jax.ShapeDtypeStruct((ROPE_PER_ROW, rows, LANES), F32)] * 2,
        compiler_params=_params(("parallel",), 0),
        name="rope_tables",
    )(pos_rep, freq)
    return cos.reshape(n_tok, LANES), sin.reshape(n_tok, LANES)


def _mem_kv_kernel(mem_ref, g_ref, w_ref, kv_ref):
    n = _rmsnorm(mem_ref[...], g_ref[...]).astype(BF16)
    kv_ref[...] = _dot(n, w_ref[...]).astype(BF16)


def _mem_kv(mem, norm_mem, w_kv):
    depth = norm_mem.shape[0]
    b, m, d = mem.shape
    n_out = w_kv.shape[-1]
    return pl.pallas_call(
        _mem_kv_kernel,
        grid=(depth, b),
        in_specs=[pl.BlockSpec((None, m, d), lambda l, i: (i, 0, 0)),
                  pl.BlockSpec((None, 1, d), lambda l, i: (l, 0, 0)),
                  pl.BlockSpec((None, d, n_out), lambda l, i: (l, 0, 0))],
        out_specs=pl.BlockSpec((None, None, m, n_out), lambda l, i: (l, i, 0, 0)),
        out_shape=jax.ShapeDtypeStruct((depth, b, m, n_out), BF16),
        compiler_params=_params(("parallel", "parallel"), 0),
        name="mem_kv",
    )(mem, norm_mem.reshape(depth, 1, d), w_kv)


MLA_QK_SCALE = (MLA_NOPE + MLA_ROPE) ** -0.5 * LOG2E
LAT_Q_END = MLA_Q_LORA
LAT_KV_END = MLA_Q_LORA + MLA_KV_LORA
LAT_EXT = LAT_KV_END + 2 * MLA_ROPE
Q_NOPE_W = MLA_HEADS * MLA_NOPE
Q_ROPE_W = MLA_HEADS * MLA_ROPE


def _mla_proj_kernel(x_ref, g_ref, wdkv_ref, gq_ref, gkv_ref, wuq_ref, wuk_ref, wvt_ref,
                     cos_ref, sin_ref, qn_ref, qp_ref, kn_ref, kpe_ref, vt_ref):
    a = _rmsnorm(x_ref[...], g_ref[...]).astype(BF16)
    lat = _dot(a, wdkv_ref[...])
    c_q = _rmsnorm(lat[:, :LAT_Q_END], gq_ref[...]).astype(BF16)
    c_kv = _rmsnorm(lat[:, LAT_Q_END:LAT_KV_END], gkv_ref[...]).astype(BF16)
    cos = cos_ref[...]
    sin = sin_ref[...]

    slab = lat[:, LAT_KV_END:LAT_EXT]
    rot = slab * cos + pltpu.roll(slab, MLA_ROPE, 1) * sin
    lane = lax.broadcasted_iota(jnp.int32, rot.shape, 1)
    k_even = jnp.where(lane < MLA_ROPE, rot, 0.0)
    k_odd = pltpu.roll(k_even, MLA_ROPE, 1)
    kpe_ref[...] = jnp.concatenate([k_even, k_odd], axis=1).astype(BF16)

    q = _dot(c_q, wuq_ref[...])
    qn_ref[...] = (q[:, :Q_NOPE_W] * MLA_QK_SCALE).astype(BF16)
    for s in range(Q_ROPE_W // LANES):
        lo = Q_NOPE_W + s * LANES
        r = q[:, lo:lo + LANES]
        r_partner = q[:, lo + Q_ROPE_W:lo + Q_ROPE_W + LANES]
        qp_ref[:, s * LANES:(s + 1) * LANES] = (
            (r * cos + r_partner * sin) * MLA_QK_SCALE).astype(BF16)

    kn_ref[...] = _dot(c_kv, wuk_ref[...]).astype(BF16)
    vt = _dot_nt(wvt_ref[...], c_kv)
    for c in range(vt_ref.shape[0]):
        vt_ref[c] = vt[:, c * TK:(c + 1) * TK].astype(BF16)


def _mla_proj(x2, g, w_dkv, g_q, g_kv, w_uq, w_ukv, cos, sin):
    t, d = x2.shape
    h = MLA_HEADS
    kpe0 = LAT_KV_END
    half = MLA_ROPE // 2
    wdkv = jnp.concatenate(
        [w_dkv, -w_dkv[:, kpe0 + half:kpe0 + MLA_ROPE], w_dkv[:, kpe0:kpe0 + half]],
        axis=1).astype(BF16)
    uq = w_uq.reshape(MLA_Q_LORA, h, MLA_NOPE + MLA_ROPE)
    uq_rope = uq[:, :, MLA_NOPE:]
    uq_partner = jnp.concatenate([-uq_rope[:, :, half:], uq_rope[:, :, :half]], axis=-1)
    wuq = jnp.concatenate(
        [uq[:, :, :MLA_NOPE].reshape(MLA_Q_LORA, Q_NOPE_W),
         uq_rope.reshape(MLA_Q_LORA, Q_ROPE_W),
         uq_partner.reshape(MLA_Q_LORA, Q_ROPE_W)], axis=1).astype(BF16)
    ukv = w_ukv.reshape(MLA_KV_LORA, h, MLA_NOPE + MLA_V)
    wuk = ukv[:, :, :MLA_NOPE].reshape(MLA_KV_LORA, h * MLA_NOPE).astype(BF16)
    wvt = ukv[:, :, MLA_NOPE:].reshape(MLA_KV_LORA, h * MLA_V).T.astype(BF16)

    tm = TM
    n_chunks = t // TK
    row = lambda i: (i, 0)
    out_shapes = [
        jax.ShapeDtypeStruct((t, Q_NOPE_W), BF16),
        jax.ShapeDtypeStruct((t, Q_ROPE_W), BF16),
        jax.ShapeDtypeStruct((t, h * MLA_NOPE), BF16),
        jax.ShapeDtypeStruct((t, 2 * LANES), BF16),
        jax.ShapeDtypeStruct((n_chunks, h * MLA_V, TK), BF16),
    ]
    return pl.pallas_call(
        _mla_proj_kernel,
        grid=(t // tm,),
        in_specs=[pl.BlockSpec((tm, d), row), _const_spec((1, d)),
                  _const_spec(wdkv.shape), _const_spec((1, MLA_Q_LORA)),
                  _const_spec((1, MLA_KV_LORA)), _const_spec(wuq.shape),
                  _const_spec(wuk.shape), _const_spec(wvt.shape),
                  pl.BlockSpec((tm, LANES), row), pl.BlockSpec((tm, LANES), row)],
        out_specs=[pl.BlockSpec((tm, Q_NOPE_W), row), pl.BlockSpec((tm, Q_ROPE_W), row),
                   pl.BlockSpec((tm, h * MLA_NOPE), row), pl.BlockSpec((tm, 2 * LANES), row),
                   pl.BlockSpec((tm // TK, h * MLA_V, TK), lambda i: (i, 0, 0))],
        out_shape=out_shapes,
        compiler_params=_params(("parallel",), 48 * 1024 * 1024),
        name="mla_proj",
    )(x2, g.reshape(1, d), wdkv, g_q.reshape(1, -1), g_kv.reshape(1, -1), wuq, wuk, wvt,
      cos, sin)


Q_TILES = 2
SUM_ROWS = 16
TQ_STEP = Q_TILES * TQ
CHUNKS_PER_ITER = 2
GROUPS_PER_BLOCK = max(Q_TILES * (Q_TILES + 1) // 2, Q_TILES * CHUNKS_PER_ITER)


def _diag_groups(step, mixer_order):
    groups = []
    for r in range(Q_TILES):
        own = [(r, step * Q_TILES + c, c == r) for c in range(r + 1)]
        groups += own if mixer_order == "any" else own[::-1]
    return groups


def _key_offset(kj):
    off = kj * TK
    return off if isinstance(off, int) else pl.multiple_of(off, TK)


def _mla_attn_kernel(qn_ref, qp_ref, kn_ref, kpe_ref, vt_ref, *refs, n_jobs):
    o_ref = refs[n_jobs]
    _run_cast_jobs(refs[:n_jobs], refs[n_jobs + 1:2 * n_jobs + 1])
    m_scr, l_scr, acc_scr, s_scr = refs[2 * n_jobs + 1:]
    step = pl.program_id(1)
    heads, hd = MLA_HEADS, MLA_NOPE
    m_scr[...] = jnp.full(m_scr.shape, NEG_BIG, F32)
    l_scr[...] = jnp.zeros(l_scr.shape, F32)
    acc_scr[...] = jnp.zeros(acc_scr.shape, F32)

    def process(groups):
        units = [(r, kj, masked, h) for r, kj, masked in groups for h in range(heads)]
        key = lax.broadcasted_iota(jnp.int32, (TK, TQ), 0)
        qry = lax.broadcasted_iota(jnp.int32, (TK, TQ), 1)
        stats = []
        for u, (r, kj, masked, h) in enumerate(units):
            koff = _key_offset(kj)
            tile = slice(r * TQ, (r + 1) * TQ)
            cols = slice(h * hd, (h + 1) * hd)
            pair = slice((h // 2) * LANES, (h // 2 + 1) * LANES)
            par = slice((h % 2) * LANES, (h % 2 + 1) * LANES)
            q_h = jnp.concatenate([qn_ref[tile, cols], qp_ref[tile, pair]], axis=1)
            k_h = jnp.concatenate([kn_ref[pl.ds(koff, TK), cols],
                                   kpe_ref[pl.ds(koff, TK), par]], axis=1)
            s = _dot_nt(k_h, q_h)
            if masked:
                s = jnp.where(key <= qry, s, NEG_BIG)
            s_scr[u] = s
            m_prev = m_scr[h:h + 1, tile]
            m_new = jnp.maximum(m_prev, jnp.max(s, axis=0, keepdims=True))
            m_scr[h:h + 1, tile] = m_new
            stats.append((m_new, jnp.exp2(m_prev - m_new)))
        p_bf = [jnp.exp2(s_scr[u] - stats[u][0]).astype(BF16) for u in range(len(units))]
        ones = jnp.ones((SUM_ROWS, TK), BF16)
        for u, (r, kj, masked, h) in enumerate(units):
            tile = slice(r * TQ, (r + 1) * TQ)
            cols = slice(h * hd, (h + 1) * hd)
            alpha = stats[u][1]
            v_ext = jnp.concatenate([vt_ref[kj, cols, :], ones], axis=0)
            pv = _dot(v_ext, p_bf[u])
            acc_scr[cols, tile] = alpha * acc_scr[cols, tile] + pv[:hd, :]
            l_scr[h:h + 1, tile] = alpha * l_scr[h:h + 1, tile] + pv[hd:hd + 1, :]

    process(_diag_groups(step, "any"))

    def body(it, carry):
        process([(r, it * CHUNKS_PER_ITER + c, False)
                 for c in range(CHUNKS_PER_ITER) for r in range(Q_TILES)])
        return carry

    lax.fori_loop(0, step * Q_TILES // CHUNKS_PER_ITER, body, 0)

    for r in range(Q_TILES):
        tile = slice(r * TQ, (r + 1) * TQ)
        for h in range(heads):
            cols = slice(h * hd, (h + 1) * hd)
            o_t = acc_scr[cols, tile] / l_scr[h:h + 1, tile]
            o_ref[tile, cols] = o_t.T.astype(BF16)


def _mla_attn(qn, qp, kn, kpe, vt, batch, seq, cast_weights, cast_layers):
    t = qn.shape[0]
    steps = seq // TQ_STEP
    job_in, job_out, job_shapes = _cast_jobs(cast_weights, cast_layers, batch * steps,
                                             lambda b, i: b * steps + i)
    nch = seq // TK
    hv = MLA_HEADS * MLA_V
    kn3 = kn.reshape(batch, seq, -1)
    kpe3 = kpe.reshape(batch, seq, -1)
    vt4 = vt.reshape(batch, nch, hv, TK)
    qrow = lambda b, i: (b * steps + i, 0)
    perb = lambda b, i: (b, 0, 0)
    n_units = MLA_HEADS * GROUPS_PER_BLOCK
    return pl.pallas_call(
        functools.partial(_mla_attn_kernel, n_jobs=len(job_in)),
        grid=(batch, steps),
        in_specs=[pl.BlockSpec((TQ_STEP, qn.shape[1]), qrow),
                  pl.BlockSpec((TQ_STEP, qp.shape[1]), qrow),
                  pl.BlockSpec((None, seq, kn3.shape[2]), perb),
                  pl.BlockSpec((None, seq, kpe3.shape[2]), perb),
                  pl.BlockSpec((None, nch, hv, TK), lambda b, i: (b, 0, 0, 0))] + job_in,
        out_specs=[pl.BlockSpec((TQ_STEP, hv), qrow)] + job_out,
        out_shape=[jax.ShapeDtypeStruct((t, hv), BF16)] + job_shapes,
        scratch_shapes=[pltpu.VMEM((MLA_HEADS, TQ_STEP), F32), pltpu.VMEM((MLA_HEADS, TQ_STEP), F32),
                        pltpu.VMEM((hv, TQ_STEP), F32), pltpu.VMEM((n_units, TK, TQ), F32)],
        compiler_params=_params(("parallel", "arbitrary"), 48 * 1024 * 1024),
        name="mla_attn",
    )(qn, qp, kn3, kpe3, vt4, *cast_weights)


SB_SCALE = SB_HEAD_DIM ** -0.5 * LOG2E
SB_W = SB_HEADS * SB_HEAD_DIM


def _sb_proj_kernel(x_ref, g_ref, w_ref, qt_ref, k_ref, vt_ref, wk_scr, wqvt_scr):
    @pl.when(pl.program_id(0) == 0)
    def _():
        wk_scr[...] = w_ref[:, SB_W:2 * SB_W].astype(BF16)
        wqvt_scr[:SB_W, :] = w_ref[:, :SB_W].T.astype(BF16)
        wqvt_scr[SB_W:, :] = w_ref[:, 2 * SB_W:].T.astype(BF16)

    a = _rmsnorm(x_ref[...], g_ref[...]).astype(BF16)
    k_ref[...] = _dot(a, wk_scr[...]).astype(BF16)
    qv_t = _dot_nt(wqvt_scr[...], a)
    qt_ref[...] = (qv_t[:SB_W, :] * SB_SCALE).astype(BF16)
    for c in range(vt_ref.shape[0]):
        vt_ref[c] = qv_t[SB_W:, c * TK:(c + 1) * TK].astype(BF16)


def _sb_proj(x2, g, w_qkv):
    t, d = x2.shape
    tm = TM
    row = lambda i: (i, 0)
    once = pl.BlockSpec(w_qkv.shape, lambda i: (0, 0), pipeline_mode=pl.Buffered(1))
    return pl.pallas_call(
        _sb_proj_kernel,
        grid=(t // tm,),
        in_specs=[pl.BlockSpec((tm, d), row), _const_spec((1, d)), once],
        out_specs=[pl.BlockSpec((SB_W, tm), lambda i: (0, i)), pl.BlockSpec((tm, SB_W), row),
                   pl.BlockSpec((tm // TK, SB_W, TK), lambda i: (i, 0, 0))],
        out_shape=[jax.ShapeDtypeStruct((SB_W, t), BF16), jax.ShapeDtypeStruct((t, SB_W), BF16),
                   jax.ShapeDtypeStruct((t // TK, SB_W, TK), BF16)],
        scratch_shapes=[pltpu.VMEM((d, SB_W), BF16), pltpu.VMEM((2 * SB_W, d), BF16)],
        compiler_params=_params(("arbitrary",), 56 * 1024 * 1024),
        name="sb_proj",
    )(x2, g.reshape(1, d), w_qkv)


SOFTPLUS_LINEAR_FROM = 64.0


def _sb_attn_kernel(q_ref, k_ref, vt_ref, *refs, n_jobs):
    o_ref = refs[n_jobs]
    _run_cast_jobs(refs[:n_jobs], refs[n_jobs + 1:2 * n_jobs + 1])
    carry_scr, acc_scr, lb_scr, top_scr, sp_scr, a_scr = refs[2 * n_jobs + 1:]
    step = pl.program_id(1)
    heads, hd = SB_HEADS, SB_HEAD_DIM
    row = lax.broadcasted_iota(jnp.int32, (TK, TK), 0)
    col = lax.broadcasted_iota(jnp.int32, (TK, TK), 1)
    suffix = jnp.where((col > row) | (col == 0), 1.0, 0.0).astype(BF16)
    first_row = lax.broadcasted_iota(jnp.int32, (SUM_ROWS, TQ), 0) == 0
    carry_scr[...] = jnp.zeros(carry_scr.shape, F32)
    acc_scr[...] = jnp.zeros(acc_scr.shape, F32)

    def process(groups):
        units = [(r, kj, masked, h) for r, kj, masked in groups for h in range(heads)]
        key = lax.broadcasted_iota(jnp.int32, (TK, TQ), 0)
        qry = lax.broadcasted_iota(jnp.int32, (TK, TQ), 1)
        for u, (r, kj, masked, h) in enumerate(units):
            koff = _key_offset(kj)
            tile = slice(r * TQ, (r + 1) * TQ)
            cols = slice(h * hd, (h + 1) * hd)
            z = _dot(k_ref[pl.ds(koff, TK), cols], q_ref[cols, tile])
            sp = jnp.where(z > SOFTPLUS_LINEAR_FROM, z, jnp.log2(1.0 + jnp.exp2(z)))
            lb_scr[u] = z - sp
            if masked:
                sp = jnp.where(key < qry, sp, 0.0)
            top_scr[u] = sp[:SUM_ROWS, :]
            sp_scr[u] = sp[SUM_ROWS:, :].astype(BF16)
        for u, (r, kj, masked, h) in enumerate(units):
            tile = slice(r * TQ, (r + 1) * TQ)
            seen = carry_scr[h:h + 1, tile]
            top = top_scr[u]
            rhs = jnp.concatenate(
                [jnp.where(first_row, seen, top).astype(BF16), sp_scr[u]], axis=0)
            later = _dot(suffix, rhs)
            seen_as_summed = seen.astype(BF16).astype(F32)
            carry_scr[h:h + 1, tile] = seen + top[0:1, :] + (later[0:1, :] - seen_as_summed)
            a = jnp.exp2(lb_scr[u] - later)
            if masked:
                a = jnp.where(key < qry, a, 0.0)
            a_scr[u] = a.astype(BF16)
        for u, (r, kj, masked, h) in enumerate(units):
            tile = slice(r * TQ, (r + 1) * TQ)
            cols = slice(h * hd, (h + 1) * hd)
            acc_scr[cols, tile] = acc_scr[cols, tile] + _dot(vt_ref[kj, cols, :], a_scr[u])

    process(_diag_groups(step, "right_to_left"))

    def body(i, carry):
        kj = step * Q_TILES - 1 - i * CHUNKS_PER_ITER
        process([(r, kj - c, False) for c in range(CHUNKS_PER_ITER) for r in range(Q_TILES)])
        return carry

    lax.fori_loop(0, step * Q_TILES // CHUNKS_PER_ITER, body, 0)

    for r in range(Q_TILES):
        tile = slice(r * TQ, (r + 1) * TQ)
        for h in range(heads):
            cols = slice(h * hd, (h + 1) * hd)
            o_ref[tile, cols] = acc_scr[cols, tile].T.astype(BF16)


def _sb_attn(q, k, vt, batch, seq, cast_weights, cast_layers):
    t = k.shape[0]
    steps = seq // TQ_STEP
    job_in, job_out, job_shapes = _cast_jobs(cast_weights, cast_layers, batch * steps,
                                             lambda b, i: b * steps + i)
    nch = seq // TK
    k3 = k.reshape(batch, seq, SB_W)
    vt4 = vt.reshape(batch, nch, SB_W, TK)
    qrow = lambda b, i: (b * steps + i, 0)
    n_units = SB_HEADS * GROUPS_PER_BLOCK
    return pl.pallas_call(
        functools.partial(_sb_attn_kernel, n_jobs=len(job_in)),
        grid=(batch, steps),
        in_specs=[pl.BlockSpec((SB_W, TQ_STEP), lambda b, i: (0, b * steps + i)),
                  pl.BlockSpec((None, seq, SB_W), lambda b, i: (b, 0, 0)),
                  pl.BlockSpec((None, nch, SB_W, TK), lambda b, i: (b, 0, 0, 0))] + job_in,
        out_specs=[pl.BlockSpec((TQ_STEP, SB_W), qrow)] + job_out,
        out_shape=[jax.ShapeDtypeStruct((t, SB_W), BF16)] + job_shapes,
        scratch_shapes=[pltpu.VMEM((SB_HEADS, TQ_STEP), F32), pltpu.VMEM((SB_W, TQ_STEP), F32),
                        pltpu.VMEM((n_units, TK, TQ), F32),
                        pltpu.VMEM((n_units, SUM_ROWS, TQ), F32),
                        pltpu.VMEM((n_units, TK - SUM_ROWS, TQ), BF16),
                        pltpu.VMEM((n_units, TK, TQ), BF16)],
        compiler_params=_params(("parallel", "arbitrary"), 48 * 1024 * 1024),
        name="sb_attn",
    )(q, k3, vt4, *cast_weights)


MEM_SCALE = MEM_HEAD_DIM ** -0.5 * LOG2E
MEM_W = MEM_HEADS * MEM_HEAD_DIM


def _cross_kernel(h_ref, o_ref, wo_ref, g_ref, wq_ref, kv_ref, wxo_ref, out_ref):
    h1 = h_ref[...] + _dot(o_ref[...], wo_ref[...])
    n = _rmsnorm(h1, g_ref[...]).astype(BF16)
    q = (_dot(n, wq_ref[...]) * MEM_SCALE).astype(BF16)
    head_cols = [slice(hh * MEM_HEAD_DIM, (hh + 1) * MEM_HEAD_DIM) for hh in range(MEM_HEADS)]
    scores = [_dot_nt(q[:, c], kv_ref[:, c]) for c in head_cols]
    probs, denoms = [], []
    for s in scores:
        p = jnp.exp2(s - jnp.max(s, axis=-1, keepdims=True))
        denoms.append(jnp.sum(p, axis=-1, keepdims=True))
        probs.append(p.astype(BF16))
    outs = []
    for c, p, denom in zip(head_cols, probs, denoms):
        v_h = kv_ref[:, MEM_W + c.start:MEM_W + c.stop]
        outs.append((_dot(p, v_h) / denom).astype(BF16))
    o = jnp.concatenate(outs, axis=1)
    out_ref[...] = h1 + _dot(o, wxo_ref[...])


def _cross(h2, o, w_o, g, w_q, kv, w_xo, layer, seq):
    t, d = h2.shape
    tm = TM
    per_batch = seq // tm
    row = lambda i: (i, 0)
    return pl.pallas_call(
        _cross_kernel,
        grid=(t // tm,),
        in_specs=[pl.BlockSpec((tm, d), row), pl.BlockSpec((tm, o.shape[1]), row),
                  _const_spec(w_o.shape), _layer_spec(g, layer), _const_spec(w_q.shape),
                  pl.BlockSpec((None, None) + kv.shape[2:],
                               lambda i: (layer, i // per_batch, 0, 0)),
                  _const_spec(w_xo.shape)],
        out_specs=pl.BlockSpec((tm, d), row),
        out_shape=jax.ShapeDtypeStruct((t, d), F32),
        compiler_params=_params(("parallel",), 48 * 1024 * 1024),
        name="cross_attn",
    )(h2, o, w_o, g, w_q, kv, w_xo)


def _mlp_kernel(h_ref, g_ref, win_ref, wout_ref, gf_ref, out_ref, *, final_norm):
    h = h_ref[...]
    n = _rmsnorm(h, g_ref[...]).astype(BF16)
    acc = h
    d_ff = win_ref.shape[1]
    for c in range(d_ff // FF_CHUNK):
        cols = slice(c * FF_CHUNK, (c + 1) * FF_CHUNK)
        u = jnp.maximum(_dot(n, win_ref[:, cols]), 0.0)
        acc = acc + _dot((u * u).astype(BF16), wout_ref[cols, :])
    if final_norm:
        acc = _rmsnorm(acc, gf_ref[...])
    out_ref[...] = acc


def _mlp(h2, g, w_in, w_out, layer, g_final, final_norm):
    t, d = h2.shape
    tm = TM
    row = lambda i: (i, 0)
    return pl.pallas_call(
        functools.partial(_mlp_kernel, final_norm=final_norm),
        grid=(t // tm,),
        in_specs=[pl.BlockSpec((tm, d), row), _layer_spec(g, layer),
                  _const_spec(w_in.shape), _const_spec(w_out.shape), _const_spec((1, d))],
        out_specs=pl.BlockSpec((tm, d), row),
        out_shape=jax.ShapeDtypeStruct((t, d), F32),
        compiler_params=_params(("parallel",), 56 * 1024 * 1024),
        name="mlp",
    )(h2, g, w_in, w_out, g_final.reshape(1, d))


def kernel(x, mem, positions, norm_mix, norm_cross, norm_mem, norm_mlp, norm_final,
           mla_w_dkv, mla_g_q, mla_g_kv, mla_w_uq, mla_w_ukv, mla_w_o,
           sb_w_qkv, sb_w_o, xa_w_q, xa_w_kv, xa_w_o, mlp_w_in, mlp_w_out):
    batch, seq, d = x.shape
    depth = norm_mix.shape[0]
    assert depth == 2 and seq % TM == 0 and TM % TK == 0 and TQ == TK and seq % TQ_STEP == 0
    assert Q_TILES % CHUNKS_PER_ITER == 0
    h = x.reshape(batch * seq, d)
    cos, sin = _rope_tables(positions)
    mem_kv = _mem_kv(mem, norm_mem, xa_w_kv.astype(BF16))

    g_cross, g_mlp = _rows(norm_cross), _rows(norm_mlp)

    for i in range(depth):
        j = i // 2
        if i % 2 == 0:
            casts = ((mla_w_o, xa_w_q, xa_w_o, mlp_w_in, mlp_w_out), (j, i, i, i, i))
            qn, qp, kn, kpe, vt = _mla_proj(h, norm_mix[i], mla_w_dkv[j], mla_g_q[j],
                                            mla_g_kv[j], mla_w_uq[j], mla_w_ukv[j], cos, sin)
            o, w_o, w_q, w_xo, w_in, w_out = _mla_attn(qn, qp, kn, kpe, vt, batch, seq, *casts)
        else:
            casts = ((sb_w_o, xa_w_q, xa_w_o, mlp_w_in, mlp_w_out), (j, i, i, i, i))
            q, k, vt = _sb_proj(h, norm_mix[i], sb_w_qkv[j])
            o, w_o, w_q, w_xo, w_in, w_out = _sb_attn(q, k, vt, batch, seq, *casts)
        h = _cross(h, o, w_o, g_cross, w_q, mem_kv, w_xo, i, seq)
        h = _mlp(h, g_mlp, w_in, w_out, i, norm_final, final_norm=(i == depth - 1))
    return h.reshape(batch, seq, d)
```

```python
import functools
import math

import jax
import jax.numpy as jnp
from jax import lax
from jax.experimental import pallas as pl
from jax.experimental.pallas import tpu as pltpu

F32 = jnp.float32
BF16 = jnp.bfloat16

D_MODEL = 1024
EPS = 1e-6
MLA_HEADS = 8
MLA_Q_LORA = 384
MLA_KV_LORA = 256
MLA_NOPE = 128
MLA_ROPE = 64
MLA_V = 128
ROPE_THETA = 10000.0
SB_HEADS = 8
SB_HEAD_DIM = 128
MEM_HEADS = 4
MEM_HEAD_DIM = 128
NEG_BIG = -1e30
LOG2E = math.log2(math.e)

LANES = 128
V7X_VMEM_BYTES = 64 * 1024 * 1024

TQ = 256
TK = 256
TM = 1024
FF_CHUNK = 1024


def _vmem_limit(nbytes):
    return int(min(max(nbytes, 32 * 1024 * 1024), V7X_VMEM_BYTES - 8 * 1024 * 1024))


def _params(semantics, vmem_bytes):
    return pltpu.CompilerParams(dimension_semantics=semantics,
                                vmem_limit_bytes=_vmem_limit(vmem_bytes))


def _const_spec(shape):
    nd = len(shape)
    return pl.BlockSpec(shape, lambda *_: (0,) * nd)


def _layer_spec(stacked, layer):
    rest = stacked.shape[1:]
    return pl.BlockSpec((None,) + rest, lambda *_: (layer,) + (0,) * len(rest))


def _rows(param):
    return param.reshape(param.shape[0], 1, param.shape[1])


def _cast_jobs(stacked_weights, layers, n_steps, slab_of):
    in_specs, out_specs, out_shapes = [], [], []
    for w, layer in zip(stacked_weights, layers):
        rows, cols = w.shape[1] // n_steps, w.shape[2]
        assert rows * n_steps == w.shape[1] and rows % 16 == 0
        in_specs.append(pl.BlockSpec((None, rows, cols),
                                     lambda *g, layer=layer: (layer, slab_of(*g), 0)))
        out_specs.append(pl.BlockSpec((rows, cols), lambda *g: (slab_of(*g), 0)))
        out_shapes.append(jax.ShapeDtypeStruct(w.shape[1:], BF16))
    return in_specs, out_specs, out_shapes


def _run_cast_jobs(src_refs, dst_refs):
    for src, dst in zip(src_refs, dst_refs):
        dst[...] = src[...].astype(BF16)


def _rmsnorm(x, g):
    return x * lax.rsqrt(jnp.mean(x * x, axis=-1, keepdims=True) + EPS) * g


def _dot(a, b):
    return jnp.dot(a, b, preferred_element_type=F32)


def _dot_nt(a, b):
    return lax.dot_general(a, b, (((1,), (1,)), ((), ())), preferred_element_type=F32)


ROPE_HALF = MLA_ROPE // 2
ROPE_PER_ROW = LANES // ROPE_HALF


def _exact_bf16_terms(x):
    hi = x.astype(BF16)
    rest = x - hi.astype(F32)
    mid = rest.astype(BF16)
    lo = (rest - mid.astype(F32)).astype(BF16)
    return hi, mid, lo


def _rope_kernel(pos_ref, freq_ref, cos_ref, sin_ref):
    ang = pos_ref[...].astype(F32) * freq_ref[...]
    src = lax.broadcasted_iota(jnp.int32, (LANES, LANES), 0)
    dst = lax.broadcasted_iota(jnp.int32, (LANES, LANES), 1)
    for table, out_ref in ((jnp.cos(ang), cos_ref), (jnp.sin(ang), sin_ref)):
        terms = _exact_bf16_terms(table)
        for j in range(ROPE_PER_ROW):
            spread = jnp.where(src == j * ROPE_HALF + (dst & (ROPE_HALF - 1)), 1.0, 0.0).astype(BF16)
            out_ref[j] = _dot(terms[0], spread) + _dot(terms[1], spread) + _dot(terms[2], spread)


def _rope_tables(positions):
    n_tok = positions.size
    inv_freq = ROPE_THETA ** (-jnp.arange(0, MLA_ROPE, 2, dtype=F32) / MLA_ROPE)
    rows = n_tok // ROPE_PER_ROW
    pos_rep = jnp.repeat(positions.reshape(ROPE_PER_ROW, rows).T, ROPE_HALF, axis=1)
    freq = jnp.tile(inv_freq, ROPE_PER_ROW).reshape(1, LANES)
    rb = 512
    out_spec = pl.BlockSpec((ROPE_PER_ROW, rb, LANES), lambda i: (0, i, 0))
    cos, sin = pl.pallas_call(
        _rope_kernel,
        grid=(rows // rb,),
        in_specs=[pl.BlockSpec((rb, LANES), lambda i: (i, 0)), _const_spec((1, LANES))],
        out_specs=[out_spec] * 2,
        out_shape=[jax.ShapeDtypeStruct((ROPE_PER_ROW, rows, LANES), F32)] * 2,
        compiler_params=_params(("parallel",), 0),
        name="rope_tables",
    )(pos_rep, freq)
    return cos.reshape(n_tok, LANES), sin.reshape(n_tok, LANES)


def _mem_kv_kernel(mem_ref, g_ref, w_ref, kv_ref):
    n = _rmsnorm(mem_ref[...], g_ref[...]).astype(BF16)
    kv_ref[...] = _dot(n, w_ref[...]).astype(BF16)


def _mem_kv(mem, norm_mem, w_kv):
    depth = norm_mem.shape[0]
    b, m, d = mem.shape
    n_out = w_kv.shape[-1]
    return pl.pallas_call(
        _mem_kv_kernel,
        grid=(depth, b),
        in_specs=[pl.BlockSpec((None, m, d), lambda l, i: (i, 0, 0)),
                  pl.BlockSpec((None, 1, d), lambda l, i: (l, 0, 0)),
                  pl.BlockSpec((None, d, n_out), lambda l, i: (l, 0, 0))],
        out_specs=pl.BlockSpec((None, None, m, n_out), lambda l, i: (l, i, 0, 0)),
        out_shape=jax.ShapeDtypeStruct((depth, b, m, n_out), BF16),
        compiler_params=_params(("parallel", "parallel"), 0),
        name="mem_kv",
    )(mem, norm_mem.reshape(depth, 1, d), w_kv)


MLA_QK_SCALE = (MLA_NOPE + MLA_ROPE) ** -0.5 * LOG2E
LAT_Q_END = MLA_Q_LORA
LAT_KV_END = MLA_Q_LORA + MLA_KV_LORA
LAT_EXT = LAT_KV_END + 2 * MLA_ROPE
Q_NOPE_W = MLA_HEADS * MLA_NOPE
Q_ROPE_W = MLA_HEADS * MLA_ROPE


def _mla_proj_kernel(x_ref, g_ref, wdkv_ref, gq_ref, gkv_ref, wuq_ref, wuk_ref, wvt_ref,
                     cos_ref, sin_ref, qn_ref, qp_ref, kn_ref, kpe_ref, vt_ref):
    a = _rmsnorm(x_ref[...], g_ref[...]).astype(BF16)
    lat = _dot(a, wdkv_ref[...])
    c_q = _rmsnorm(lat[:, :LAT_Q_END], gq_ref[...]).astype(BF16)
    c_kv = _rmsnorm(lat[:, LAT_Q_END:LAT_KV_END], gkv_ref[...]).astype(BF16)
    cos = cos_ref[...]
    sin = sin_ref[...]

    slab = lat[:, LAT_KV_END:LAT_EXT]
    rot = slab * cos + pltpu.roll(slab, MLA_ROPE, 1) * sin
    lane = lax.broadcasted_iota(jnp.int32, rot.shape, 1)
    k_even = jnp.where(lane < MLA_ROPE, rot, 0.0)
    k_odd = pltpu.roll(k_even, MLA_ROPE, 1)
    kpe_ref[...] = jnp.concatenate([k_even, k_odd], axis=1).astype(BF16)

    q = _dot(c_q, wuq_ref[...])
    qn_ref[...] = (q[:, :Q_NOPE_W] * MLA_QK_SCALE).astype(BF16)
    for s in range(Q_ROPE_W // LANES):
        lo = Q_NOPE_W + s * LANES
        r = q[:, lo:lo + LANES]
        r_partner = q[:, lo + Q_ROPE_W:lo + Q_ROPE_W + LANES]
        qp_ref[:, s * LANES:(s + 1) * LANES] = (
            (r * cos + r_partner * sin) * MLA_QK_SCALE).astype(BF16)

    kn_ref[...] = _dot(c_kv, wuk_ref[...]).astype(BF16)
    vt = _dot_nt(wvt_ref[...], c_kv)
    for c in range(vt_ref.shape[0]):
        vt_ref[c] = vt[:, c * TK:(c + 1) * TK].astype(BF16)


def _mla_proj(x2, g, w_dkv, g_q, g_kv, w_uq, w_ukv, cos, sin):
    t, d = x2.shape
    h = MLA_HEADS
    kpe0 = LAT_KV_END
    half = MLA_ROPE // 2
    wdkv = jnp.concatenate(
        [w_dkv, -w_dkv[:, kpe0 + half:kpe0 + MLA_ROPE], w_dkv[:, kpe0:kpe0 + half]],
        axis=1).astype(BF16)
    uq = w_uq.reshape(MLA_Q_LORA, h, MLA_NOPE + MLA_ROPE)
    uq_rope = uq[:, :, MLA_NOPE:]
    uq_partner = jnp.concatenate([-uq_rope[:, :, half:], uq_rope[:, :, :half]], axis=-1)
    wuq = jnp.concatenate(
        [uq[:, :, :MLA_NOPE].reshape(MLA_Q_LORA, Q_NOPE_W),
         uq_rope.reshape(MLA_Q_LORA, Q_ROPE_W),
         uq_partner.reshape(MLA_Q_LORA, Q_ROPE_W)], axis=1).astype(BF16)
    ukv = w_ukv.reshape(MLA_KV_LORA, h, MLA_NOPE + MLA_V)
    wuk = ukv[:, :, :MLA_NOPE].reshape(MLA_KV_LORA, h * MLA_NOPE).astype(BF16)
    wvt = ukv[:, :, MLA_NOPE:].reshape(MLA_KV_LORA, h * MLA_V).T.astype(BF16)

    tm = TM
    n_chunks = t // TK
    row = lambda i: (i, 0)
    out_shapes = [
        jax.ShapeDtypeStruct((t, Q_NOPE_W), BF16),
        jax.ShapeDtypeStruct((t, Q_ROPE_W), BF16),
        jax.ShapeDtypeStruct((t, h * MLA_NOPE), BF16),
        jax.ShapeDtypeStruct((t, 2 * LANES), BF16),
        jax.ShapeDtypeStruct((n_chunks, h * MLA_V, TK), BF16),
    ]
    return pl.pallas_call(
        _mla_proj_kernel,
        grid=(t // tm,),
        in_specs=[pl.BlockSpec((tm, d), row), _const_spec((1, d)),
                  _const_spec(wdkv.shape), _const_spec((1, MLA_Q_LORA)),
                  _const_spec((1, MLA_KV_LORA)), _const_spec(wuq.shape),
                  _const_spec(wuk.shape), _const_spec(wvt.shape),
                  pl.BlockSpec((tm, LANES), row), pl.BlockSpec((tm, LANES), row)],
        out_specs=[pl.BlockSpec((tm, Q_NOPE_W), row), pl.BlockSpec((tm, Q_ROPE_W), row),
                   pl.BlockSpec((tm, h * MLA_NOPE), row), pl.BlockSpec((tm, 2 * LANES), row),
                   pl.BlockSpec((tm // TK, h * MLA_V, TK), lambda i: (i, 0, 0))],
        out_shape=out_shapes,
        compiler_params=_params(("parallel",), 48 * 1024 * 1024),
        name="mla_proj",
    )(x2, g.reshape(1, d), wdkv, g_q.reshape(1, -1), g_kv.reshape(1, -1), wuq, wuk, wvt,
      cos, sin)


Q_TILES = 2
SUM_ROWS = 16
TQ_STEP = Q_TILES * TQ
CHUNKS_PER_ITER = 2
GROUPS_PER_BLOCK = max(Q_TILES * (Q_TILES + 1) // 2, Q_TILES * CHUNKS_PER_ITER)


def _diag_groups(step, mixer_order):
    groups = []
    for r in range(Q_TILES):
        own = [(r, step * Q_TILES + c, c == r) for c in range(r + 1)]
        groups += own if mixer_order == "any" else own[::-1]
    return groups


def _key_offset(kj):
    off = kj * TK
    return off if isinstance(off, int) else pl.multiple_of(off, TK)


def _mla_attn_kernel(qn_ref, qp_ref, kn_ref, kpe_ref, vt_ref, *refs, n_jobs):
    o_ref = refs[n_jobs]
    _run_cast_jobs(refs[:n_jobs], refs[n_jobs + 1:2 * n_jobs + 1])
    m_scr, l_scr, acc_scr, s_scr = refs[2 * n_jobs + 1:]
    step = pl.program_id(1)
    heads, hd = MLA_HEADS, MLA_NOPE
    m_scr[...] = jnp.full(m_scr.shape, NEG_BIG, F32)
    l_scr[...] = jnp.zeros(l_scr.shape, F32)
    acc_scr[...] = jnp.zeros(acc_scr.shape, F32)

    def process(groups):
        units = [(r, kj, masked, h) for r, kj, masked in groups for h in range(heads)]
        key = lax.broadcasted_iota(jnp.int32, (TK, TQ), 0)
        qry = lax.broadcasted_iota(jnp.int32, (TK, TQ), 1)
        stats = []
        for u, (r, kj, masked, h) in enumerate(units):
            koff = _key_offset(kj)
            tile = slice(r * TQ, (r + 1) * TQ)
            cols = slice(h * hd, (h + 1) * hd)
            pair = slice((h // 2) * LANES, (h // 2 + 1) * LANES)
            par = slice((h % 2) * LANES, (h % 2 + 1) * LANES)
            q_h = jnp.concatenate([qn_ref[tile, cols], qp_ref[tile, pair]], axis=1)
            k_h = jnp.concatenate([kn_ref[pl.ds(koff, TK), cols],
                                   kpe_ref[pl.ds(koff, TK), par]], axis=1)
            s = _dot_nt(k_h, q_h)
            if masked:
                s = jnp.where(key <= qry, s, NEG_BIG)
            s_scr[u] = s
            m_prev = m_scr[h:h + 1, tile]
            m_new = jnp.maximum(m_prev, jnp.max(s, axis=0, keepdims=True))
            m_scr[h:h + 1, tile] = m_new
            stats.append((m_new, jnp.exp2(m_prev - m_new)))
        p_bf = [jnp.exp2(s_scr[u] - stats[u][0]).astype(BF16) for u in range(len(units))]
        ones = jnp.ones((SUM_ROWS, TK), BF16)
        for u, (r, kj, masked, h) in enumerate(units):
            tile = slice(r * TQ, (r + 1) * TQ)
            cols = slice(h * hd, (h + 1) * hd)
            alpha = stats[u][1]
            v_ext = jnp.concatenate([vt_ref[kj, cols, :], ones], axis=0)
            pv = _dot(v_ext, p_bf[u])
            acc_scr[cols, tile] = alpha * acc_scr[cols, tile] + pv[:hd, :]
            l_scr[h:h + 1, tile] = alpha * l_scr[h:h + 1, tile] + pv[hd:hd + 1, :]

    process(_diag_groups(step, "any"))

    def body(it, carry):
        process([(r, it * CHUNKS_PER_ITER + c, False)
                 for c in range(CHUNKS_PER_ITER) for r in range(Q_TILES)])
        return carry

    lax.fori_loop(0, step * Q_TILES // CHUNKS_PER_ITER, body, 0)

    for r in range(Q_TILES):
        tile = slice(r * TQ, (r + 1) * TQ)
        for h in range(heads):
            cols = slice(h * hd, (h + 1) * hd)
            o_t = acc_scr[cols, tile] / l_scr[h:h + 1, tile]
            o_ref[tile, cols] = o_t.T.astype(BF16)


def _mla_attn(qn, qp, kn, kpe, vt, batch, seq, cast_weights, cast_layers):
    t = qn.shape[0]
    steps = seq // TQ_STEP
    job_in, job_out, job_shapes = _cast_jobs(cast_weights, cast_layers, batch * steps,
                                             lambda b, i: b * steps + i)
    nch = seq // TK
    hv = MLA_HEADS * MLA_V
    kn3 = kn.reshape(batch, seq, -1)
    kpe3 = kpe.reshape(batch, seq, -1)
    vt4 = vt.reshape(batch, nch, hv, TK)
    qrow = lambda b, i: (b * steps + i, 0)
    perb = lambda b, i: (b, 0, 0)
    n_units = MLA_HEADS * GROUPS_PER_BLOCK
    return pl.pallas_call(
        functools.partial(_mla_attn_kernel, n_jobs=len(job_in)),
        grid=(batch, steps),
        in_specs=[pl.BlockSpec((TQ_STEP, qn.shape[1]), qrow),
                  pl.BlockSpec((TQ_STEP, qp.shape[1]), qrow),
                  pl.BlockSpec((None, seq, kn3.shape[2]), perb),
                  pl.BlockSpec((None, seq, kpe3.shape[2]), perb),
                  pl.BlockSpec((None, nch, hv, TK), lambda b, i: (b, 0, 0, 0))] + job_in,
        out_specs=[pl.BlockSpec((TQ_STEP, hv), qrow)] + job_out,
        out_shape=[jax.ShapeDtypeStruct((t, hv), BF16)] + job_shapes,
        scratch_shapes=[pltpu.VMEM((MLA_HEADS, TQ_STEP), F32), pltpu.VMEM((MLA_HEADS, TQ_STEP), F32),
                        pltpu.VMEM((hv, TQ_STEP), F32), pltpu.VMEM((n_units, TK, TQ), F32)],
        compiler_params=_params(("parallel", "arbitrary"), 48 * 1024 * 1024),
        name="mla_attn",
    )(qn, qp, kn3, kpe3, vt4, *cast_weights)


SB_SCALE = SB_HEAD_DIM ** -0.5 * LOG2E
SB_W = SB_HEADS * SB_HEAD_DIM


def _sb_proj_kernel(x_ref, g_ref, w_ref, qt_ref, k_ref, vt_ref, wk_scr, wqvt_scr):
    @pl.when(pl.program_id(0) == 0)
    def _():
        wk_scr[...] = w_ref[:, SB_W:2 * SB_W].astype(BF16)
        wqvt_scr[:SB_W, :] = w_ref[:, :SB_W].T.astype(BF16)
        wqvt_scr[SB_W:, :] = w_ref[:, 2 * SB_W:].T.astype(BF16)

    a = _rmsnorm(x_ref[...], g_ref[...]).astype(BF16)
    k_ref[...] = _dot(a, wk_scr[...]).astype(BF16)
    qv_t = _dot_nt(wqvt_scr[...], a)
    qt_ref[...] = (qv_t[:SB_W, :] * SB_SCALE).astype(BF16)
    for c in range(vt_ref.shape[0]):
        vt_ref[c] = qv_t[SB_W:, c * TK:(c + 1) * TK].astype(BF16)


def _sb_proj(x2, g, w_qkv):
    t, d = x2.shape
    tm = TM
    row = lambda i: (i, 0)
    once = pl.BlockSpec(w_qkv.shape, lambda i: (0, 0), pipeline_mode=pl.Buffered(1))
    return pl.pallas_call(
        _sb_proj_kernel,
        grid=(t // tm,),
        in_specs=[pl.BlockSpec((tm, d), row), _const_spec((1, d)), once],
        out_specs=[pl.BlockSpec((SB_W, tm), lambda i: (0, i)), pl.BlockSpec((tm, SB_W), row),
                   pl.BlockSpec((tm // TK, SB_W, TK), lambda i: (i, 0, 0))],
        out_shape=[jax.ShapeDtypeStruct((SB_W, t), BF16), jax.ShapeDtypeStruct((t, SB_W), BF16),
                   jax.ShapeDtypeStruct((t // TK, SB_W, TK), BF16)],
        scratch_shapes=[pltpu.VMEM((d, SB_W), BF16), pltpu.VMEM((2 * SB_W, d), BF16)],
        compiler_params=_params(("arbitrary",), 56 * 1024 * 1024),
        name="sb_proj",
    )(x2, g.reshape(1, d), w_qkv)


SOFTPLUS_LINEAR_FROM = 64.0


def _sb_attn_kernel(q_ref, k_ref, vt_ref, *refs, n_jobs):
    o_ref = refs[n_jobs]
    _run_cast_jobs(refs[:n_jobs], refs[n_jobs + 1:2 * n_jobs + 1])
    carry_scr, acc_scr, lb_scr, top_scr, sp_scr, a_scr = refs[2 * n_jobs + 1:]
    step = pl.program_id(1)
    heads, hd = SB_HEADS, SB_HEAD_DIM
    row = lax.broadcasted_iota(jnp.int32, (TK, TK), 0)
    col = lax.broadcasted_iota(jnp.int32, (TK, TK), 1)
    suffix = jnp.where((col > row) | (col == 0), 1.0, 0.0).astype(BF16)
    first_row = lax.broadcasted_iota(jnp.int32, (SUM_ROWS, TQ), 0) == 0
    carry_scr[...] = jnp.zeros(carry_scr.shape, F32)
    acc_scr[...] = jnp.zeros(acc_scr.shape, F32)

    def process(groups):
        units = [(r, kj, masked, h) for r, kj, masked in groups for h in range(heads)]
        key = lax.broadcasted_iota(jnp.int32, (TK, TQ), 0)
        qry = lax.broadcasted_iota(jnp.int32, (TK, TQ), 1)
        for u, (r, kj, masked, h) in enumerate(units):
            koff = _key_offset(kj)
            tile = slice(r * TQ, (r + 1) * TQ)
            cols = slice(h * hd, (h + 1) * hd)
            z = _dot(k_ref[pl.ds(koff, TK), cols], q_ref[cols, tile])
            if masked:
                z = jnp.where(key < qry, z, NEG_BIG)
            sp = jnp.where(z > SOFTPLUS_LINEAR_FROM, z, jnp.log2(1.0 + jnp.exp2(z)))
            lb_scr[u] = z - sp
            top_scr[u] = sp[:SUM_ROWS, :]
            sp_scr[u] = sp[SUM_ROWS:, :].astype(BF16)
        for u, (r, kj, masked, h) in enumerate(units):
            tile = slice(r * TQ, (r + 1) * TQ)
            seen = carry_scr[h:h + 1, tile]
            top = top_scr[u]
            rhs = jnp.concatenate(
                [jnp.where(first_row, seen, top).astype(BF16), sp_scr[u]], axis=0)
            later = _dot(suffix, rhs)
            seen_as_summed = seen.astype(BF16).astype(F32)
            carry_scr[h:h + 1, tile] = seen + top[0:1, :] + (later[0:1, :] - seen_as_summed)
            a_scr[u] = jnp.exp2(lb_scr[u] - later).astype(BF16)
        for u, (r, kj, masked, h) in enumerate(units):
            tile = slice(r * TQ, (r + 1) * TQ)
            cols = slice(h * hd, (h + 1) * hd)
            acc_scr[cols, tile] = acc_scr[cols, tile] + _dot(vt_ref[kj, cols, :], a_scr[u])

    process(_diag_groups(step, "right_to_left"))

    def body(i, carry):
        kj = step * Q_TILES - 1 - i * CHUNKS_PER_ITER
        process([(r, kj - c, False) for c in range(CHUNKS_PER_ITER) for r in range(Q_TILES)])
        return carry

    lax.fori_loop(0, step * Q_TILES // CHUNKS_PER_ITER, body, 0)

    for r in range(Q_TILES):
        tile = slice(r * TQ, (r + 1) * TQ)
        for h in range(heads):
            cols = slice(h * hd, (h + 1) * hd)
            o_ref[tile, cols] = acc_scr[cols, tile].T.astype(BF16)


def _sb_attn(q, k, vt, batch, seq, cast_weights, cast_layers):
    t = k.shape[0]
    steps = seq // TQ_STEP
    job_in, job_out, job_shapes = _cast_jobs(cast_weights, cast_layers, batch * steps,
                                             lambda b, i: b * steps + i)
    nch = seq // TK
    k3 = k.reshape(batch, seq, SB_W)
    vt4 = vt.reshape(batch, nch, SB_W, TK)
    qrow = lambda b, i: (b * steps + i, 0)
    n_units = SB_HEADS * GROUPS_PER_BLOCK
    return pl.pallas_call(
        functools.partial(_sb_attn_kernel, n_jobs=len(job_in)),
        grid=(batch, steps),
        in_specs=[pl.BlockSpec((SB_W, TQ_STEP), lambda b, i: (0, b * steps + i)),
                  pl.BlockSpec((None, seq, SB_W), lambda b, i: (b, 0, 0)),
                  pl.BlockSpec((None, nch, SB_W, TK), lambda b, i: (b, 0, 0, 0))] + job_in,
        out_specs=[pl.BlockSpec((TQ_STEP, SB_W), qrow)] + job_out,
        out_shape=[jax.ShapeDtypeStruct((t, SB_W), BF16)] + job_shapes,
        scratch_shapes=[pltpu.VMEM((SB_HEADS, TQ_STEP), F32), pltpu.VMEM((SB_W, TQ_STEP), F32),
                        pltpu.VMEM((n_units, TK, TQ), F32),
                        pltpu.VMEM((n_units, SUM_ROWS, TQ), F32),
                        pltpu.VMEM((n_units, TK - SUM_ROWS, TQ), BF16),
                        pltpu.VMEM((n_units, TK, TQ), BF16)],
        compiler_params=_params(("parallel", "arbitrary"), 48 * 1024 * 1024),
        name="sb_attn",
    )(q, k3, vt4, *cast_weights)


MEM_SCALE = MEM_HEAD_DIM ** -0.5 * LOG2E
MEM_W = MEM_HEADS * MEM_HEAD_DIM


def _cross_kernel(h_ref, o_ref, wo_ref, g_ref, wq_ref, kv_ref, wxo_ref, out_ref):
    h1 = h_ref[...] + _dot(o_ref[...], wo_ref[...])
    n = _rmsnorm(h1, g_ref[...]).astype(BF16)
    q = (_dot(n, wq_ref[...]) * MEM_SCALE).astype(BF16)
    head_cols = [slice(hh * MEM_HEAD_DIM, (hh + 1) * MEM_HEAD_DIM) for hh in range(MEM_HEADS)]
    scores = [_dot_nt(q[:, c], kv_ref[:, c]) for c in head_cols]
    probs, denoms = [], []
    for s in scores:
        p = jnp.exp2(s - jnp.max(s, axis=-1, keepdims=True))
        denoms.append(jnp.sum(p, axis=-1, keepdims=True))
        probs.append(p.astype(BF16))
    outs = []
    for c, p, denom in zip(head_cols, probs, denoms):
        v_h = kv_ref[:, MEM_W + c.start:MEM_W + c.stop]
        outs.append((_dot(p, v_h) / denom).astype(BF16))
    o = jnp.concatenate(outs, axis=1)
    out_ref[...] = h1 + _dot(o, wxo_ref[...])


def _cross(h2, o, w_o, g, w_q, kv, w_xo, layer, seq):
    t, d = h2.shape
    tm = TM
    per_batch = seq // tm
    row = lambda i: (i, 0)
    return pl.pallas_call(
        _cross_kernel,
        grid=(t // tm,),
        in_specs=[pl.BlockSpec((tm, d), row), pl.BlockSpec((tm, o.shape[1]), row),
                  _const_spec(w_o.shape), _layer_spec(g, layer), _const_spec(w_q.shape),
                  pl.BlockSpec((None, None) + kv.shape[2:],
                               lambda i: (layer, i // per_batch, 0, 0)),
                  _const_spec(w_xo.shape)],
        out_specs=pl.BlockSpec((tm, d), row),
        out_shape=jax.ShapeDtypeStruct((t, d), F32),
        compiler_params=_params(("parallel",), 48 * 1024 * 1024),
        name="cross_attn",
    )(h2, o, w_o, g, w_q, kv, w_xo)


def _mlp_kernel(h_ref, g_ref, win_ref, wout_ref, gf_ref, out_ref, *, final_norm):
    h = h_ref[...]
    n = _rmsnorm(h, g_ref[...]).astype(BF16)
    acc = h
    d_ff = win_ref.shape[1]
    for c in range(d_ff // FF_CHUNK):
        cols = slice(c * FF_CHUNK, (c + 1) * FF_CHUNK)
        u = jnp.maximum(_dot(n, win_ref[:, cols]), 0.0)
        acc = acc + _dot((u * u).astype(BF16), wout_ref[cols, :])
    if final_norm:
        acc = _rmsnorm(acc, gf_ref[...])
    out_ref[...] = acc


def _mlp(h2, g, w_in, w_out, layer, g_final, final_norm):
    t, d = h2.shape
    tm = TM
    row = lambda i: (i, 0)
    return pl.pallas_call(
        functools.partial(_mlp_kernel, final_norm=final_norm),
        grid=(t // tm,),
        in_specs=[pl.BlockSpec((tm, d), row), _layer_spec(g, layer),
                  _const_spec(w_in.shape), _const_spec(w_out.shape), _const_spec((1, d))],
        out_specs=pl.BlockSpec((tm, d), row),
        out_shape=jax.ShapeDtypeStruct((t, d), F32),
        compiler_params=_params(("parallel",), 56 * 1024 * 1024),
        name="mlp",
    )(h2, g, w_in, w_out, g_final.reshape(1, d))


def kernel(x, mem, positions, norm_mix, norm_cross, norm_mem, norm_mlp, norm_final,
           mla_w_dkv, mla_g_q, mla_g_kv, mla_w_uq, mla_w_ukv, mla_w_o,
           sb_w_qkv, sb_w_o, xa_w_q, xa_w_kv, xa_w_o, mlp_w_in, mlp_w_out):
    batch, seq, d = x.shape
    depth = norm_mix.shape[0]
    assert depth == 2 and seq % TM == 0 and TM % TK == 0 and TQ == TK and seq % TQ_STEP == 0
    assert Q_TILES % CHUNKS_PER_ITER == 0
    h = x.reshape(batch * seq, d)
    cos, sin = _rope_tables(positions)
    mem_kv = _mem_kv(mem, norm_mem, xa_w_kv.astype(BF16))

    g_cross, g_mlp = _rows(norm_cross), _rows(norm_mlp)

    for i in range(depth):
        j = i // 2
        if i % 2 == 0:
            casts = ((mla_w_o, xa_w_q, xa_w_o, mlp_w_in, mlp_w_out), (j, i, i, i, i))
            qn, qp, kn, kpe, vt = _mla_proj(h, norm_mix[i], mla_w_dkv[j], mla_g_q[j],
                                            mla_g_kv[j], mla_w_uq[j], mla_w_ukv[j], cos, sin)
            o, w_o, w_q, w_xo, w_in, w_out = _mla_attn(qn, qp, kn, kpe, vt, batch, seq, *casts)
        else:
            casts = ((sb_w_o, xa_w_q, xa_w_o, mlp_w_in, mlp_w_out), (j, i, i, i, i))
            q, k, vt = _sb_proj(h, norm_mix[i], sb_w_qkv[j])
            o, w_o, w_q, w_xo, w_in, w_out = _sb_attn(q, k, vt, batch, seq, *casts)
        h = _cross(h, o, w_o, g_cross, w_q, mem_kv, w_xo, i, seq)
        h = _mlp(h, g_mlp, w_in, w_out, i, norm_final, final_norm=(i == depth - 1))
    return h.reshape(batch, seq, d)
```

```python
import functools
import math

import jax
import jax.numpy as jnp
from jax import lax
from jax.experimental import pallas as pl
from jax.experimental.pallas import tpu as pltpu

F32 = jnp.float32
BF16 = jnp.bfloat16

D_MODEL = 1024
EPS = 1e-6
MLA_HEADS = 8
MLA_Q_LORA = 384
MLA_KV_LORA = 256
MLA_NOPE = 128
MLA_ROPE = 64
MLA_V = 128
ROPE_THETA = 10000.0
SB_HEADS = 8
SB_HEAD_DIM = 128
MEM_HEADS = 4
MEM_HEAD_DIM = 128
NEG_BIG = -1e30
LOG2E = math.log2(math.e)

LANES = 128
V7X_VMEM_BYTES = 64 * 1024 * 1024

TQ = 256
TK = 256
TM = 1024
FF_CHUNK = 1024


def _vmem_limit(nbytes):
    return int(min(max(nbytes, 32 * 1024 * 1024), V7X_VMEM_BYTES - 8 * 1024 * 1024))


def _params(semantics, vmem_bytes):
    return pltpu.CompilerParams(dimension_semantics=semantics,
                                vmem_limit_bytes=_vmem_limit(vmem_bytes))


def _const_spec(shape):
    nd = len(shape)
    return pl.BlockSpec(shape, lambda *_: (0,) * nd)


def _layer_spec(stacked, layer):
    rest = stacked.shape[1:]
    return pl.BlockSpec((None,) + rest, lambda *_: (layer,) + (0,) * len(rest))


def _rows(param):
    return param.reshape(param.shape[0], 1, param.shape[1])


def _cast_jobs(stacked_weights, layers, n_steps, slab_of):
    in_specs, out_specs, out_shapes = [], [], []
    for w, layer in zip(stacked_weights, layers):
        rows, cols = w.shape[1] // n_steps, w.shape[2]
        assert rows * n_steps == w.shape[1] and rows % 16 == 0
        in_specs.append(pl.BlockSpec((None, rows, cols),
                                     lambda *g, layer=layer: (layer, slab_of(*g), 0)))
        out_specs.append(pl.BlockSpec((rows, cols), lambda *g: (slab_of(*g), 0)))
        out_shapes.append(jax.ShapeDtypeStruct(w.shape[1:], BF16))
    return in_specs, out_specs, out_shapes


def _run_cast_jobs(src_refs, dst_refs):
    for src, dst in zip(src_refs, dst_refs):
        dst[...] = src[...].astype(BF16)


def _rmsnorm(x, g):
    return x * lax.rsqrt(jnp.mean(x * x, axis=-1, keepdims=True) + EPS) * g


def _dot(a, b):
    return jnp.dot(a, b, preferred_element_type=F32)


def _dot_nt(a, b):
    return lax.dot_general(a, b, (((1,), (1,)), ((), ())), preferred_element_type=F32)


ROPE_HALF = MLA_ROPE // 2
ROPE_PER_ROW = LANES // ROPE_HALF


def _exact_bf16_terms(x):
    hi = x.astype(BF16)
    rest = x - hi.astype(F32)
    mid = rest.astype(BF16)
    lo = (rest - mid.astype(F32)).astype(BF16)
    return hi, mid, lo


def _rope_kernel(pos_ref, freq_ref, cos_ref, sin_ref):
    ang = pos_ref[...].astype(F32) * freq_ref[...]
    src = lax.broadcasted_iota(jnp.int32, (LANES, LANES), 0)
    dst = lax.broadcasted_iota(jnp.int32, (LANES, LANES), 1)
    for table, out_ref in ((jnp.cos(ang), cos_ref), (jnp.sin(ang), sin_ref)):
        terms = _exact_bf16_terms(table)
        for j in range(ROPE_PER_ROW):
            spread = jnp.where(src == j * ROPE_HALF + (dst & (ROPE_HALF - 1)), 1.0, 0.0).astype(BF16)
            out_ref[j] = _dot(terms[0], spread) + _dot(terms[1], spread) + _dot(terms[2], spread)


def _rope_tables(positions):
    n_tok = positions.size
    inv_freq = ROPE_THETA ** (-jnp.arange(0, MLA_ROPE, 2, dtype=F32) / MLA_ROPE)
    rows = n_tok // ROPE_PER_ROW
    pos_rep = jnp.repeat(positions.reshape(ROPE_PER_ROW, rows).T, ROPE_HALF, axis=1)
    freq = jnp.tile(inv_freq, ROPE_PER_ROW).reshape(1, LANES)
    rb = 512
    out_spec = pl.BlockSpec((ROPE_PER_ROW, rb, LANES), lambda i: (0, i, 0))
    cos, sin = pl.pallas_call(
        _rope_kernel,
        grid=(rows // rb,),
        in_specs=[pl.BlockSpec((rb, LANES), lambda i: (i, 0)), _const_spec((1, LANES))],
        out_specs=[out_spec] * 2,
        out_shape=[jax.ShapeDtypeStruct((ROPE_PER_ROW, rows, LANES), F32)] * 2,
        compiler_params=_params(("parallel",), 0),
        name="rope_tables",
    )(pos_rep, freq)
    return cos.reshape(n_tok, LANES), sin.reshape(n_tok, LANES)


def _mem_kv_kernel(mem_ref, g_ref, w_ref, kv_ref):
    n = _rmsnorm(mem_ref[...], g_ref[...]).astype(BF16)
    kv_ref[...] = _dot(n, w_ref[...]).astype(BF16)


def _mem_kv(mem, norm_mem, w_kv):
    depth = norm_mem.shape[0]
    b, m, d = mem.shape
    n_out = w_kv.shape[-1]
    return pl.pallas_call(
        _mem_kv_kernel,
        grid=(depth, b),
        in_specs=[pl.BlockSpec((None, m, d), lambda l, i: (i, 0, 0)),
                  pl.BlockSpec((None, 1, d), lambda l, i: (l, 0, 0)),
                  pl.BlockSpec((None, d, n_out), lambda l, i: (l, 0, 0))],
        out_specs=pl.BlockSpec((None, None, m, n_out), lambda l, i: (l, i, 0, 0)),
        out_shape=jax.ShapeDtypeStruct((depth, b, m, n_out), BF16),
        compiler_params=_params(("parallel", "parallel"), 0),
        name="mem_kv",
    )(mem, norm_mem.reshape(depth, 1, d), w_kv)


MLA_QK_SCALE = (MLA_NOPE + MLA_ROPE) ** -0.5 * LOG2E
LAT_Q_END = MLA_Q_LORA
LAT_KV_END = MLA_Q_LORA + MLA_KV_LORA
LAT_EXT = LAT_KV_END + 2 * MLA_ROPE
Q_NOPE_W = MLA_HEADS * MLA_NOPE
Q_ROPE_W = MLA_HEADS * MLA_ROPE


def _mla_proj_kernel(x_ref, g_ref, wdkv_ref, gq_ref, gkv_ref, wuq_ref, wuk_ref, wvt_ref,
                     cos_ref, sin_ref, qn_ref, qp_ref, kn_ref, kpe_ref, vt_ref):
    a = _rmsnorm(x_ref[...], g_ref[...]).astype(BF16)
    lat = _dot(a, wdkv_ref[...])
    c_q = _rmsnorm(lat[:, :LAT_Q_END], gq_ref[...]).astype(BF16)
    c_kv = _rmsnorm(lat[:, LAT_Q_END:LAT_KV_END], gkv_ref[...]).astype(BF16)
    cos = cos_ref[...]
    sin = sin_ref[...]

    slab = lat[:, LAT_KV_END:LAT_EXT]
    rot = slab * cos + pltpu.roll(slab, MLA_ROPE, 1) * sin
    lane = lax.broadcasted_iota(jnp.int32, rot.shape, 1)
    k_even = jnp.where(lane < MLA_ROPE, rot, 0.0)
    k_odd = pltpu.roll(k_even, MLA_ROPE, 1)
    kpe_ref[...] = jnp.concatenate([k_even, k_odd], axis=1).astype(BF16)

    q = _dot(c_q, wuq_ref[...])
    qn_ref[...] = (q[:, :Q_NOPE_W] * MLA_QK_SCALE).astype(BF16)
    for s in range(Q_ROPE_W // LANES):
        lo = Q_NOPE_W + s * LANES
        r = q[:, lo:lo + LANES]
        r_partner = q[:, lo + Q_ROPE_W:lo + Q_ROPE_W + LANES]
        qp_ref[:, s * LANES:(s + 1) * LANES] = (
            (r * cos + r_partner * sin) * MLA_QK_SCALE).astype(BF16)

    kn_ref[...] = _dot(c_kv, wuk_ref[...]).astype(BF16)
    vt = _dot_nt(wvt_ref[...], c_kv)
    for c in range(vt_ref.shape[0]):
        vt_ref[c] = vt[:, c * TK:(c + 1) * TK].astype(BF16)


def _mla_proj(x2, g, w_dkv, g_q, g_kv, w_uq, w_ukv, cos, sin):
    t, d = x2.shape
    h = MLA_HEADS
    kpe0 = LAT_KV_END
    half = MLA_ROPE // 2
    wdkv = jnp.concatenate(
        [w_dkv, -w_dkv[:, kpe0 + half:kpe0 + MLA_ROPE], w_dkv[:, kpe0:kpe0 + half]],
        axis=1).astype(BF16)
    uq = w_uq.reshape(MLA_Q_LORA, h, MLA_NOPE + MLA_ROPE)
    uq_rope = uq[:, :, MLA_NOPE:]
    uq_partner = jnp.concatenate([-uq_rope[:, :, half:], uq_rope[:, :, :half]], axis=-1)
    wuq = jnp.concatenate(
        [uq[:, :, :MLA_NOPE].reshape(MLA_Q_LORA, Q_NOPE_W),
         uq_rope.reshape(MLA_Q_LORA, Q_ROPE_W),
         uq_partner.reshape(MLA_Q_LORA, Q_ROPE_W)], axis=1).astype(BF16)
    ukv = w_ukv.reshape(MLA_KV_LORA, h, MLA_NOPE + MLA_V)
    wuk = ukv[:, :, :MLA_NOPE].reshape(MLA_KV_LORA, h * MLA_NOPE).astype(BF16)
    wvt = ukv[:, :, MLA_NOPE:].reshape(MLA_KV_LORA, h * MLA_V).T.astype(BF16)

    tm = TM
    n_chunks = t // TK
    row = lambda i: (i, 0)
    out_shapes = [
        jax.ShapeDtypeStruct((t, Q_NOPE_W), BF16),
        jax.ShapeDtypeStruct((t, Q_ROPE_W), BF16),
        jax.ShapeDtypeStruct((t, h * MLA_NOPE), BF16),
        jax.ShapeDtypeStruct((t, 2 * LANES), BF16),
        jax.ShapeDtypeStruct((n_chunks, h * MLA_V, TK), BF16),
    ]
    return pl.pallas_call(
        _mla_proj_kernel,
        grid=(t // tm,),
        in_specs=[pl.BlockSpec((tm, d), row), _const_spec((1, d)),
                  _const_spec(wdkv.shape), _const_spec((1, MLA_Q_LORA)),
                  _const_spec((1, MLA_KV_LORA)), _const_spec(wuq.shape),
                  _const_spec(wuk.shape), _const_spec(wvt.shape),
                  pl.BlockSpec((tm, LANES), row), pl.BlockSpec((tm, LANES), row)],
        out_specs=[pl.BlockSpec((tm, Q_NOPE_W), row), pl.BlockSpec((tm, Q_ROPE_W), row),
                   pl.BlockSpec((tm, h * MLA_NOPE), row), pl.BlockSpec((tm, 2 * LANES), row),
                   pl.BlockSpec((tm // TK, h * MLA_V, TK), lambda i: (i, 0, 0))],
        out_shape=out_shapes,
        compiler_params=_params(("parallel",), 48 * 1024 * 1024),
        name="mla_proj",
    )(x2, g.reshape(1, d), wdkv, g_q.reshape(1, -1), g_kv.reshape(1, -1), wuq, wuk, wvt,
      cos, sin)


Q_TILES = 2
SUM_ROWS = 16
TQ_STEP = Q_TILES * TQ
CHUNKS_PER_ITER = 2
DIAG_GROUPS = Q_TILES * (Q_TILES + 1) // 2
MLA_GROUPS_PER_BLOCK = max(DIAG_GROUPS, Q_TILES * CHUNKS_PER_ITER)
SB_GROUPS_PER_BLOCK = max(DIAG_GROUPS, Q_TILES)


def _diag_groups(step, mixer_order):
    groups = []
    for r in range(Q_TILES):
        own = [(r, step * Q_TILES + c, c == r) for c in range(r + 1)]
        groups += own if mixer_order == "any" else own[::-1]
    return groups


def _key_offset(kj):
    off = kj * TK
    return off if isinstance(off, int) else pl.multiple_of(off, TK)


def _mla_attn_kernel(qn_ref, qp_ref, kn_ref, kpe_ref, vt_ref, *refs, n_jobs):
    o_ref = refs[n_jobs]
    _run_cast_jobs(refs[:n_jobs], refs[n_jobs + 1:2 * n_jobs + 1])
    m_scr, l_scr, acc_scr, s_scr = refs[2 * n_jobs + 1:]
    step = pl.program_id(1)
    heads, hd = MLA_HEADS, MLA_NOPE
    m_scr[...] = jnp.full(m_scr.shape, NEG_BIG, F32)
    l_scr[...] = jnp.zeros(l_scr.shape, F32)
    acc_scr[...] = jnp.zeros(acc_scr.shape, F32)

    def process(groups):
        units = [(r, kj, masked, h) for r, kj, masked in groups for h in range(heads)]
        key = lax.broadcasted_iota(jnp.int32, (TK, TQ), 0)
        qry = lax.broadcasted_iota(jnp.int32, (TK, TQ), 1)
        stats = []
        for u, (r, kj, masked, h) in enumerate(units):
            koff = _key_offset(kj)
            tile = slice(r * TQ, (r + 1) * TQ)
            cols = slice(h * hd, (h + 1) * hd)
            pair = slice((h // 2) * LANES, (h // 2 + 1) * LANES)
            par = slice((h % 2) * LANES, (h % 2 + 1) * LANES)
            q_h = jnp.concatenate([qn_ref[tile, cols], qp_ref[tile, pair]], axis=1)
            k_h = jnp.concatenate([kn_ref[pl.ds(koff, TK), cols],
                                   kpe_ref[pl.ds(koff, TK), par]], axis=1)
            s = _dot_nt(k_h, q_h)
            if masked:
                s = jnp.where(key <= qry, s, NEG_BIG)
            s_scr[u] = s
            m_prev = m_scr[h:h + 1, tile]
            m_new = jnp.maximum(m_prev, jnp.max(s, axis=0, keepdims=True))
            m_scr[h:h + 1, tile] = m_new
            stats.append((m_new, jnp.exp2(m_prev - m_new)))
        p_bf = [jnp.exp2(s_scr[u] - stats[u][0]).astype(BF16) for u in range(len(units))]
        ones = jnp.ones((SUM_ROWS, TK), BF16)
        for u, (r, kj, masked, h) in enumerate(units):
            tile = slice(r * TQ, (r + 1) * TQ)
            cols = slice(h * hd, (h + 1) * hd)
            alpha = stats[u][1]
            v_ext = jnp.concatenate([vt_ref[kj, cols, :], ones], axis=0)
            pv = _dot(v_ext, p_bf[u])
            acc_scr[cols, tile] = alpha * acc_scr[cols, tile] + pv[:hd, :]
            l_scr[h:h + 1, tile] = alpha * l_scr[h:h + 1, tile] + pv[hd:hd + 1, :]

    process(_diag_groups(step, "any"))

    def body(it, carry):
        process([(r, it * CHUNKS_PER_ITER + c, False)
                 for c in range(CHUNKS_PER_ITER) for r in range(Q_TILES)])
        return carry

    lax.fori_loop(0, step * Q_TILES // CHUNKS_PER_ITER, body, 0)

    for r in range(Q_TILES):
        tile = slice(r * TQ, (r + 1) * TQ)
        for h in range(heads):
            cols = slice(h * hd, (h + 1) * hd)
            o_t = acc_scr[cols, tile] / l_scr[h:h + 1, tile]
            o_ref[tile, cols] = o_t.T.astype(BF16)


def _mla_attn(qn, qp, kn, kpe, vt, batch, seq, cast_weights, cast_layers):
    t = qn.shape[0]
    steps = seq // TQ_STEP
    job_in, job_out, job_shapes = _cast_jobs(cast_weights, cast_layers, batch * steps,
                                             lambda b, i: b * steps + i)
    nch = seq // TK
    hv = MLA_HEADS * MLA_V
    kn3 = kn.reshape(batch, seq, -1)
    kpe3 = kpe.reshape(batch, seq, -1)
    vt4 = vt.reshape(batch, nch, hv, TK)
    qrow = lambda b, i: (b * steps + i, 0)
    perb = lambda b, i: (b, 0, 0)
    n_units = MLA_HEADS * MLA_GROUPS_PER_BLOCK
    return pl.pallas_call(
        functools.partial(_mla_attn_kernel, n_jobs=len(job_in)),
        grid=(batch, steps),
        in_specs=[pl.BlockSpec((TQ_STEP, qn.shape[1]), qrow),
                  pl.BlockSpec((TQ_STEP, qp.shape[1]), qrow),
                  pl.BlockSpec((None, seq, kn3.shape[2]), perb),
                  pl.BlockSpec((None, seq, kpe3.shape[2]), perb),
                  pl.BlockSpec((None, nch, hv, TK), lambda b, i: (b, 0, 0, 0))] + job_in,
        out_specs=[pl.BlockSpec((TQ_STEP, hv), qrow)] + job_out,
        out_shape=[jax.ShapeDtypeStruct((t, hv), BF16)] + job_shapes,
        scratch_shapes=[pltpu.VMEM((MLA_HEADS, TQ_STEP), F32), pltpu.VMEM((MLA_HEADS, TQ_STEP), F32),
                        pltpu.VMEM((hv, TQ_STEP), F32), pltpu.VMEM((n_units, TK, TQ), F32)],
        compiler_params=_params(("parallel", "arbitrary"), 48 * 1024 * 1024),
        name="mla_attn",
    )(qn, qp, kn3, kpe3, vt4, *cast_weights)


SB_SCALE = SB_HEAD_DIM ** -0.5 * LOG2E
SB_W = SB_HEADS * SB_HEAD_DIM


def _sb_proj_kernel(x_ref, g_ref, w_ref, qt_ref, k_ref, vt_ref, wk_scr, wqvt_scr):
    @pl.when(pl.program_id(0) == 0)
    def _():
        wk_scr[...] = w_ref[:, SB_W:2 * SB_W].astype(BF16)
        wqvt_scr[:SB_W, :] = w_ref[:, :SB_W].T.astype(BF16)
        wqvt_scr[SB_W:, :] = w_ref[:, 2 * SB_W:].T.astype(BF16)

    a = _rmsnorm(x_ref[...], g_ref[...]).astype(BF16)
    k_ref[...] = _dot(a, wk_scr[...]).astype(BF16)
    qv_t = _dot_nt(wqvt_scr[...], a)
    qt_ref[...] = (qv_t[:SB_W, :] * SB_SCALE).astype(BF16)
    for c in range(vt_ref.shape[0]):
        vt_ref[c] = qv_t[SB_W:, c * TK:(c + 1) * TK].astype(BF16)


def _sb_proj(x2, g, w_qkv):
    t, d = x2.shape
    tm = TM
    row = lambda i: (i, 0)
    once = pl.BlockSpec(w_qkv.shape, lambda i: (0, 0), pipeline_mode=pl.Buffered(1))
    return pl.pallas_call(
        _sb_proj_kernel,
        grid=(t // tm,),
        in_specs=[pl.BlockSpec((tm, d), row), _const_spec((1, d)), once],
        out_specs=[pl.BlockSpec((SB_W, tm), lambda i: (0, i)), pl.BlockSpec((tm, SB_W), row),
                   pl.BlockSpec((tm // TK, SB_W, TK), lambda i: (i, 0, 0))],
        out_shape=[jax.ShapeDtypeStruct((SB_W, t), BF16), jax.ShapeDtypeStruct((t, SB_W), BF16),
                   jax.ShapeDtypeStruct((t // TK, SB_W, TK), BF16)],
        scratch_shapes=[pltpu.VMEM((d, SB_W), BF16), pltpu.VMEM((2 * SB_W, d), BF16)],
        compiler_params=_params(("arbitrary",), 56 * 1024 * 1024),
        name="sb_proj",
    )(x2, g.reshape(1, d), w_qkv)


SOFTPLUS_LINEAR_FROM = 64.0
SB_DEAD_BITS = 160.0


def _sb_attn_kernel(q_ref, k_ref, vt_ref, *refs, n_jobs):
    o_ref = refs[n_jobs]
    _run_cast_jobs(refs[:n_jobs], refs[n_jobs + 1:2 * n_jobs + 1])
    carry_scr, acc_scr, lb_scr, top_scr, sp_scr, a_scr = refs[2 * n_jobs + 1:]
    step = pl.program_id(1)
    heads, hd = SB_HEADS, SB_HEAD_DIM
    row = lax.broadcasted_iota(jnp.int32, (TK, TK), 0)
    col = lax.broadcasted_iota(jnp.int32, (TK, TK), 1)
    suffix = jnp.where((col > row) | (col == 0), 1.0, 0.0).astype(BF16)
    first_row = lax.broadcasted_iota(jnp.int32, (SUM_ROWS, TQ), 0) == 0
    carry_scr[...] = jnp.zeros(carry_scr.shape, F32)
    acc_scr[...] = jnp.zeros(acc_scr.shape, F32)

    def process(groups):
        units = [(r, kj, masked, h) for r, kj, masked in groups for h in range(heads)]
        key = lax.broadcasted_iota(jnp.int32, (TK, TQ), 0)
        qry = lax.broadcasted_iota(jnp.int32, (TK, TQ), 1)
        for u, (r, kj, masked, h) in enumerate(units):
            koff = _key_offset(kj)
            tile = slice(r * TQ, (r + 1) * TQ)
            cols = slice(h * hd, (h + 1) * hd)
            z = _dot(k_ref[pl.ds(koff, TK), cols], q_ref[cols, tile])
            if masked:
                z = jnp.where(key < qry, z, NEG_BIG)
            sp = jnp.where(z > SOFTPLUS_LINEAR_FROM, z, jnp.log2(1.0 + jnp.exp2(z)))
            lb_scr[u] = z - sp
            top_scr[u] = sp[:SUM_ROWS, :]
            sp_scr[u] = sp[SUM_ROWS:, :].astype(BF16)
        for u, (r, kj, masked, h) in enumerate(units):
            tile = slice(r * TQ, (r + 1) * TQ)
            seen = carry_scr[h:h + 1, tile]
            top = top_scr[u]
            rhs = jnp.concatenate(
                [jnp.where(first_row, seen, top).astype(BF16), sp_scr[u]], axis=0)
            later = _dot(suffix, rhs)
            seen_as_summed = seen.astype(BF16).astype(F32)
            carry_scr[h:h + 1, tile] = seen + top[0:1, :] + (later[0:1, :] - seen_as_summed)
            a_scr[u] = jnp.exp2(lb_scr[u] - later).astype(BF16)
        for u, (r, kj, masked, h) in enumerate(units):
            tile = slice(r * TQ, (r + 1) * TQ)
            cols = slice(h * hd, (h + 1) * hd)
            acc_scr[cols, tile] = acc_scr[cols, tile] + _dot(vt_ref[kj, cols, :], a_scr[u])

    process(_diag_groups(step, "right_to_left"))

    def more_to_do(state):
        i, floor = state
        return jnp.logical_and(i < step * Q_TILES, floor < SB_DEAD_BITS)

    def body(state):
        i, _ = state
        kj = step * Q_TILES - 1 - i
        process([(r, kj, False) for r in range(Q_TILES)])
        return i + 1, jnp.min(carry_scr[...])

    lax.while_loop(more_to_do, body, (jnp.int32(0), jnp.min(carry_scr[...])))

    for r in range(Q_TILES):
        tile = slice(r * TQ, (r + 1) * TQ)
        for h in range(heads):
            cols = slice(h * hd, (h + 1) * hd)
            o_ref[tile, cols] = acc_scr[cols, tile].T.astype(BF16)


def _sb_attn(q, k, vt, batch, seq, cast_weights, cast_layers):
    t = k.shape[0]
    steps = seq // TQ_STEP
    job_in, job_out, job_shapes = _cast_jobs(cast_weights, cast_layers, batch * steps,
                                             lambda b, i: b * steps + i)
    nch = seq // TK
    k3 = k.reshape(batch, seq, SB_W)
    vt4 = vt.reshape(batch, nch, SB_W, TK)
    qrow = lambda b, i: (b * steps + i, 0)
    n_units = SB_HEADS * SB_GROUPS_PER_BLOCK
    return pl.pallas_call(
        functools.partial(_sb_attn_kernel, n_jobs=len(job_in)),
        grid=(batch, steps),
        in_specs=[pl.BlockSpec((SB_W, TQ_STEP), lambda b, i: (0, b * steps + i)),
                  pl.BlockSpec((None, seq, SB_W), lambda b, i: (b, 0, 0)),
                  pl.BlockSpec((None, nch, SB_W, TK), lambda b, i: (b, 0, 0, 0))] + job_in,
        out_specs=[pl.BlockSpec((TQ_STEP, SB_W), qrow)] + job_out,
        out_shape=[jax.ShapeDtypeStruct((t, SB_W), BF16)] + job_shapes,
        scratch_shapes=[pltpu.VMEM((SB_HEADS, TQ_STEP), F32), pltpu.VMEM((SB_W, TQ_STEP), F32),
                        pltpu.VMEM((n_units, TK, TQ), F32),
                        pltpu.VMEM((n_units, SUM_ROWS, TQ), F32),
                        pltpu.VMEM((n_units, TK - SUM_ROWS, TQ), BF16),
                        pltpu.VMEM((n_units, TK, TQ), BF16)],
        compiler_params=_params(("parallel", "arbitrary"), 48 * 1024 * 1024),
        name="sb_attn",
    )(q, k3, vt4, *cast_weights)


MEM_SCALE = MEM_HEAD_DIM ** -0.5 * LOG2E
MEM_W = MEM_HEADS * MEM_HEAD_DIM


def _cross_kernel(h_ref, o_ref, wo_ref, g_ref, wq_ref, kv_ref, wxo_ref, out_ref):
    h1 = h_ref[...] + _dot(o_ref[...], wo_ref[...])
    n = _rmsnorm(h1, g_ref[...]).astype(BF16)
    q = (_dot(n, wq_ref[...]) * MEM_SCALE).astype(BF16)
    head_cols = [slice(hh * MEM_HEAD_DIM, (hh + 1) * MEM_HEAD_DIM) for hh in range(MEM_HEADS)]
    scores = [_dot_nt(q[:, c], kv_ref[:, c]) for c in head_cols]
    probs, denoms = [], []
    for s in scores:
        p = jnp.exp2(s - jnp.max(s, axis=-1, keepdims=True))
        denoms.append(jnp.sum(p, axis=-1, keepdims=True))
        probs.append(p.astype(BF16))
    outs = []
    for c, p, denom in zip(head_cols, probs, denoms):
        v_h = kv_ref[:, MEM_W + c.start:MEM_W + c.stop]
        outs.append((_dot(p, v_h) / denom).astype(BF16))
    o = jnp.concatenate(outs, axis=1)
    out_ref[...] = h1 + _dot(o, wxo_ref[...])


def _cross(h2, o, w_o, g, w_q, kv, w_xo, layer, seq):
    t, d = h2.shape
    tm = TM
    per_batch = seq // tm
    row = lambda i: (i, 0)
    return pl.pallas_call(
        _cross_kernel,
        grid=(t // tm,),
        in_specs=[pl.BlockSpec((tm, d), row), pl.BlockSpec((tm, o.shape[1]), row),
                  _const_spec(w_o.shape), _layer_spec(g, layer), _const_spec(w_q.shape),
                  pl.BlockSpec((None, None) + kv.shape[2:],
                               lambda i: (layer, i // per_batch, 0, 0)),
                  _const_spec(w_xo.shape)],
        out_specs=pl.BlockSpec((tm, d), row),
        out_shape=jax.ShapeDtypeStruct((t, d), F32),
        compiler_params=_params(("parallel",), 48 * 1024 * 1024),
        name="cross_attn",
    )(h2, o, w_o, g, w_q, kv, w_xo)


def _mlp_kernel(h_ref, g_ref, win_ref, wout_ref, gf_ref, out_ref, *, final_norm):
    h = h_ref[...]
    n = _rmsnorm(h, g_ref[...]).astype(BF16)
    acc = h
    d_ff = win_ref.shape[1]
    for c in range(d_ff // FF_CHUNK):
        cols = slice(c * FF_CHUNK, (c + 1) * FF_CHUNK)
        u = jnp.maximum(_dot(n, win_ref[:, cols]), 0.0)
        acc = acc + _dot((u * u).astype(BF16), wout_ref[cols, :])
    if final_norm:
        acc = _rmsnorm(acc, gf_ref[...])
    out_ref[...] = acc


def _mlp(h2, g, w_in, w_out, layer, g_final, final_norm):
    t, d = h2.shape
    tm = TM
    row = lambda i: (i, 0)
    return pl.pallas_call(
        functools.partial(_mlp_kernel, final_norm=final_norm),
        grid=(t // tm,),
        in_specs=[pl.BlockSpec((tm, d), row), _layer_spec(g, layer),
                  _const_spec(w_in.shape), _const_spec(w_out.shape), _const_spec((1, d))],
        out_specs=pl.BlockSpec((tm, d), row),
        out_shape=jax.ShapeDtypeStruct((t, d), F32),
        compiler_params=_params(("parallel",), 56 * 1024 * 1024),
        name="mlp",
    )(h2, g, w_in, w_out, g_final.reshape(1, d))


def kernel(x, mem, positions, norm_mix, norm_cross, norm_mem, norm_mlp, norm_final,
           mla_w_dkv, mla_g_q, mla_g_kv, mla_w_uq, mla_w_ukv, mla_w_o,
           sb_w_qkv, sb_w_o, xa_w_q, xa_w_kv, xa_w_o, mlp_w_in, mlp_w_out):
    batch, seq, d = x.shape
    depth = norm_mix.shape[0]
    assert depth == 2 and seq % TM == 0 and TM % TK == 0 and TQ == TK and seq % TQ_STEP == 0
    assert Q_TILES % CHUNKS_PER_ITER == 0
    h = x.reshape(batch * seq, d)
    cos, sin = _rope_tables(positions)
    mem_kv = _mem_kv(mem, norm_mem, xa_w_kv.astype(BF16))

    g_cross, g_mlp = _rows(norm_cross), _rows(norm_mlp)

    for i in range(depth):
        j = i // 2
        if i % 2 == 0:
            casts = ((mla_w_o, xa_w_q, xa_w_o, mlp_w_in, mlp_w_out), (j, i, i, i, i))
            qn, qp, kn, kpe, vt = _mla_proj(h, norm_mix[i], mla_w_dkv[j], mla_g_q[j],
                                            mla_g_kv[j], mla_w_uq[j], mla_w_ukv[j], cos, sin)
            o, w_o, w_q, w_xo, w_in, w_out = _mla_attn(qn, qp, kn, kpe, vt, batch, seq, *casts)
        else:
            casts = ((sb_w_o, xa_w_q, xa_w_o, mlp_w_in, mlp_w_out), (j, i, i, i, i))
            q, k, vt = _sb_proj(h, norm_mix[i], sb_w_qkv[j])
            o, w_o, w_q, w_xo, w_in, w_out = _sb_attn(q, k, vt, batch, seq, *casts)
        h = _cross(h, o, w_o, g_cross, w_q, mem_kv, w_xo, i, seq)
        h = _mlp(h, g_mlp, w_in, w_out, i, norm_final, final_norm=(i == depth - 1))
    return h.reshape(batch, seq, d)
```

```python
import functools
import math

import jax
import jax.numpy as jnp
from jax import lax
from jax.experimental import pallas as pl
from jax.experimental.pallas import tpu as pltpu

F32 = jnp.float32
BF16 = jnp.bfloat16

D_MODEL = 1024
EPS = 1e-6
MLA_HEADS = 8
MLA_Q_LORA = 384
MLA_KV_LORA = 256
MLA_NOPE = 128
MLA_ROPE = 64
MLA_V = 128
ROPE_THETA = 10000.0
SB_HEADS = 8
SB_HEAD_DIM = 128
MEM_HEADS = 4
MEM_HEAD_DIM = 128
NEG_BIG = -1e30
LOG2E = math.log2(math.e)

LANES = 128
V7X_VMEM_BYTES = 64 * 1024 * 1024

TQ = 256
TK = 256
TM = 1024
FF_CHUNK = 1024


def _vmem_limit(nbytes):
    return int(min(max(nbytes, 32 * 1024 * 1024), V7X_VMEM_BYTES - 8 * 1024 * 1024))


def _params(semantics, vmem_bytes):
    return pltpu.CompilerParams(dimension_semantics=semantics,
                                vmem_limit_bytes=_vmem_limit(vmem_bytes))


def _const_spec(shape):
    nd = len(shape)
    return pl.BlockSpec(shape, lambda *_: (0,) * nd)


def _layer_spec(stacked, layer):
    rest = stacked.shape[1:]
    return pl.BlockSpec((None,) + rest, lambda *_: (layer,) + (0,) * len(rest))


def _rows(param):
    return param.reshape(param.shape[0], 1, param.shape[1])


def _cast_jobs(stacked_weights, layers, n_steps, slab_of):
    in_specs, out_specs, out_shapes = [], [], []
    for w, layer in zip(stacked_weights, layers):
        rows, cols = w.shape[1] // n_steps, w.shape[2]
        assert rows * n_steps == w.shape[1] and rows % 16 == 0
        in_specs.append(pl.BlockSpec((None, rows, cols),
                                     lambda *g, layer=layer: (layer, slab_of(*g), 0)))
        out_specs.append(pl.BlockSpec((rows, cols), lambda *g: (slab_of(*g), 0)))
        out_shapes.append(jax.ShapeDtypeStruct(w.shape[1:], BF16))
    return in_specs, out_specs, out_shapes


def _run_cast_jobs(src_refs, dst_refs):
    for src, dst in zip(src_refs, dst_refs):
        dst[...] = src[...].astype(BF16)


def _rmsnorm(x, g):
    return x * lax.rsqrt(jnp.mean(x * x, axis=-1, keepdims=True) + EPS) * g


def _dot(a, b):
    return jnp.dot(a, b, preferred_element_type=F32)


def _dot_nt(a, b):
    return lax.dot_general(a, b, (((1,), (1,)), ((), ())), preferred_element_type=F32)


ROPE_HALF = MLA_ROPE // 2
ROPE_PER_ROW = LANES // ROPE_HALF


def _exact_bf16_terms(x):
    hi = x.astype(BF16)
    rest = x - hi.astype(F32)
    mid = rest.astype(BF16)
    lo = (rest - mid.astype(F32)).astype(BF16)
    return hi, mid, lo


def _rope_kernel(pos_ref, freq_ref, cos_ref, sin_ref):
    ang = pos_ref[...].astype(F32) * freq_ref[...]
    src = lax.broadcasted_iota(jnp.int32, (LANES, LANES), 0)
    dst = lax.broadcasted_iota(jnp.int32, (LANES, LANES), 1)
    for table, out_ref in ((jnp.cos(ang), cos_ref), (jnp.sin(ang), sin_ref)):
        terms = _exact_bf16_terms(table)
        for j in range(ROPE_PER_ROW):
            spread = jnp.where(src == j * ROPE_HALF + (dst & (ROPE_HALF - 1)), 1.0, 0.0).astype(BF16)
            out_ref[j] = _dot(terms[0], spread) + _dot(terms[1], spread) + _dot(terms[2], spread)


def _rope_tables(positions):
    n_tok = positions.size
    inv_freq = ROPE_THETA ** (-jnp.arange(0, MLA_ROPE, 2, dtype=F32) / MLA_ROPE)
    rows = n_tok // ROPE_PER_ROW
    pos_rep = jnp.repeat(positions.reshape(ROPE_PER_ROW, rows).T, ROPE_HALF, axis=1)
    freq = jnp.tile(inv_freq, ROPE_PER_ROW).reshape(1, LANES)
    rb = 512
    out_spec = pl.BlockSpec((ROPE_PER_ROW, rb, LANES), lambda i: (0, i, 0))
    cos, sin = pl.pallas_call(
        _rope_kernel,
        grid=(rows // rb,),
        in_specs=[pl.BlockSpec((rb, LANES), lambda i: (i, 0)), _const_spec((1, LANES))],
        out_specs=[out_spec] * 2,
        out_shape=[jax.ShapeDtypeStruct((ROPE_PER_ROW, rows, LANES), F32)] * 2,
        compiler_params=_params(("parallel",), 0),
        name="rope_tables",
    )(pos_rep, freq)
    return cos.reshape(n_tok, LANES), sin.reshape(n_tok, LANES)


def _mem_kv_kernel(mem_ref, g_ref, w_ref, kv_ref):
    n = _rmsnorm(mem_ref[...], g_ref[...]).astype(BF16)
    kv_ref[...] = _dot(n, w_ref[...]).astype(BF16)


def _mem_kv(mem, norm_mem, w_kv):
    depth = norm_mem.shape[0]
    b, m, d = mem.shape
    n_out = w_kv.shape[-1]
    return pl.pallas_call(
        _mem_kv_kernel,
        grid=(depth, b),
        in_specs=[pl.BlockSpec((None, m, d), lambda l, i: (i, 0, 0)),
                  pl.BlockSpec((None, 1, d), lambda l, i: (l, 0, 0)),
                  pl.BlockSpec((None, d, n_out), lambda l, i: (l, 0, 0))],
        out_specs=pl.BlockSpec((None, None, m, n_out), lambda l, i: (l, i, 0, 0)),
        out_shape=jax.ShapeDtypeStruct((depth, b, m, n_out), BF16),
        compiler_params=_params(("parallel", "parallel"), 0),
        name="mem_kv",
    )(mem, norm_mem.reshape(depth, 1, d), w_kv)


MLA_QK_SCALE = (MLA_NOPE + MLA_ROPE) ** -0.5 * LOG2E
LAT_Q_END = MLA_Q_LORA
LAT_KV_END = MLA_Q_LORA + MLA_KV_LORA
LAT_EXT = LAT_KV_END + 2 * MLA_ROPE
Q_NOPE_W = MLA_HEADS * MLA_NOPE
Q_ROPE_W = MLA_HEADS * MLA_ROPE


def _mla_proj_kernel(x_ref, g_ref, wdkv_ref, gq_ref, gkv_ref, wuq_ref, wuk_ref, wvt_ref,
                     cos_ref, sin_ref, qn_ref, qp_ref, kn_ref, kpe_ref, vt_ref):
    a = _rmsnorm(x_ref[...], g_ref[...]).astype(BF16)
    lat = _dot(a, wdkv_ref[...])
    c_q = _rmsnorm(lat[:, :LAT_Q_END], gq_ref[...]).astype(BF16)
    c_kv = _rmsnorm(lat[:, LAT_Q_END:LAT_KV_END], gkv_ref[...]).astype(BF16)
    cos = cos_ref[...]
    sin = sin_ref[...]

    slab = lat[:, LAT_KV_END:LAT_EXT]
    rot = slab * cos + pltpu.roll(slab, MLA_ROPE, 1) * sin
    lane = lax.broadcasted_iota(jnp.int32, rot.shape, 1)
    k_even = jnp.where(lane < MLA_ROPE, rot, 0.0)
    k_odd = pltpu.roll(k_even, MLA_ROPE, 1)
    kpe_ref[...] = jnp.concatenate([k_even, k_odd], axis=1).astype(BF16)

    q = _dot(c_q, wuq_ref[...])
    qn_ref[...] = (q[:, :Q_NOPE_W] * MLA_QK_SCALE).astype(BF16)
    for s in range(Q_ROPE_W // LANES):
        lo = Q_NOPE_W + s * LANES
        r = q[:, lo:lo + LANES]
        r_partner = q[:, lo + Q_ROPE_W:lo + Q_ROPE_W + LANES]
        qp_ref[:, s * LANES:(s + 1) * LANES] = (
            (r * cos + r_partner * sin) * MLA_QK_SCALE).astype(BF16)

    kn_ref[...] = _dot(c_kv, wuk_ref[...]).astype(BF16)
    vt = _dot_nt(wvt_ref[...], c_kv)
    for c in range(vt_ref.shape[0]):
        vt_ref[c] = vt[:, c * TK:(c + 1) * TK].astype(BF16)


def _mla_proj(x2, g, w_dkv, g_q, g_kv, w_uq, w_ukv, cos, sin):
    t, d = x2.shape
    h = MLA_HEADS
    kpe0 = LAT_KV_END
    half = MLA_ROPE // 2
    wdkv = jnp.concatenate(
        [w_dkv, -w_dkv[:, kpe0 + half:kpe0 + MLA_ROPE], w_dkv[:, kpe0:kpe0 + half]],
        axis=1).astype(BF16)
    uq = w_uq.reshape(MLA_Q_LORA, h, MLA_NOPE + MLA_ROPE)
    uq_rope = uq[:, :, MLA_NOPE:]
    uq_partner = jnp.concatenate([-uq_rope[:, :, half:], uq_rope[:, :, :half]], axis=-1)
    wuq = jnp.concatenate(
        [uq[:, :, :MLA_NOPE].reshape(MLA_Q_LORA, Q_NOPE_W),
         uq_rope.reshape(MLA_Q_LORA, Q_ROPE_W),
         uq_partner.reshape(MLA_Q_LORA, Q_ROPE_W)], axis=1).astype(BF16)
    ukv = w_ukv.reshape(MLA_KV_LORA, h, MLA_NOPE + MLA_V)
    wuk = ukv[:, :, :MLA_NOPE].reshape(MLA_KV_LORA, h * MLA_NOPE).astype(BF16)
    wvt = ukv[:, :, MLA_NOPE:].reshape(MLA_KV_LORA, h * MLA_V).T.astype(BF16)

    tm = TM
    n_chunks = t // TK
    row = lambda i: (i, 0)
    out_shapes = [
        jax.ShapeDtypeStruct((t, Q_NOPE_W), BF16),
        jax.ShapeDtypeStruct((t, Q_ROPE_W), BF16),
        jax.ShapeDtypeStruct((t, h * MLA_NOPE), BF16),
        jax.ShapeDtypeStruct((t, 2 * LANES), BF16),
        jax.ShapeDtypeStruct((n_chunks, h * MLA_V, TK), BF16),
    ]
    return pl.pallas_call(
        _mla_proj_kernel,
        grid=(t // tm,),
        in_specs=[pl.BlockSpec((tm, d), row), _const_spec((1, d)),
                  _const_spec(wdkv.shape), _const_spec((1, MLA_Q_LORA)),
                  _const_spec((1, MLA_KV_LORA)), _const_spec(wuq.shape),
                  _const_spec(wuk.shape), _const_spec(wvt.shape),
                  pl.BlockSpec((tm, LANES), row), pl.BlockSpec((tm, LANES), row)],
        out_specs=[pl.BlockSpec((tm, Q_NOPE_W), row), pl.BlockSpec((tm, Q_ROPE_W), row),
                   pl.BlockSpec((tm, h * MLA_NOPE), row), pl.BlockSpec((tm, 2 * LANES), row),
                   pl.BlockSpec((tm // TK, h * MLA_V, TK), lambda i: (i, 0, 0))],
        out_shape=out_shapes,
        compiler_params=_params(("parallel",), 48 * 1024 * 1024),
        name="mla_proj",
    )(x2, g.reshape(1, d), wdkv, g_q.reshape(1, -1), g_kv.reshape(1, -1), wuq, wuk, wvt,
      cos, sin)


Q_TILES = 2
SUM_ROWS = 16
TQ_STEP = Q_TILES * TQ
CHUNKS_PER_ITER = 2
DIAG_GROUPS = Q_TILES * (Q_TILES + 1) // 2
MLA_GROUPS_PER_BLOCK = max(DIAG_GROUPS, Q_TILES * CHUNKS_PER_ITER)
SB_GROUPS_PER_BLOCK = Q_TILES


def _diag_groups(step, mixer_order):
    groups = []
    for r in range(Q_TILES):
        own = [(r, step * Q_TILES + c, c == r) for c in range(r + 1)]
        groups += own if mixer_order == "any" else own[::-1]
    return groups


def _key_offset(kj):
    off = kj * TK
    return off if isinstance(off, int) else pl.multiple_of(off, TK)


def _mla_attn_kernel(qn_ref, qp_ref, kn_ref, kpe_ref, vt_ref, *refs, n_jobs):
    o_ref = refs[n_jobs]
    _run_cast_jobs(refs[:n_jobs], refs[n_jobs + 1:2 * n_jobs + 1])
    m_scr, l_scr, acc_scr, s_scr = refs[2 * n_jobs + 1:]
    step = pl.program_id(1)
    heads, hd = MLA_HEADS, MLA_NOPE
    m_scr[...] = jnp.full(m_scr.shape, NEG_BIG, F32)
    l_scr[...] = jnp.zeros(l_scr.shape, F32)
    acc_scr[...] = jnp.zeros(acc_scr.shape, F32)

    def process(groups):
        units = [(r, kj, masked, h) for r, kj, masked in groups for h in range(heads)]
        key = lax.broadcasted_iota(jnp.int32, (TK, TQ), 0)
        qry = lax.broadcasted_iota(jnp.int32, (TK, TQ), 1)
        stats = []
        for u, (r, kj, masked, h) in enumerate(units):
            koff = _key_offset(kj)
            tile = slice(r * TQ, (r + 1) * TQ)
            cols = slice(h * hd, (h + 1) * hd)
            pair = slice((h // 2) * LANES, (h // 2 + 1) * LANES)
            par = slice((h % 2) * LANES, (h % 2 + 1) * LANES)
            q_h = jnp.concatenate([qn_ref[tile, cols], qp_ref[tile, pair]], axis=1)
            k_h = jnp.concatenate([kn_ref[pl.ds(koff, TK), cols],
                                   kpe_ref[pl.ds(koff, TK), par]], axis=1)
            s = _dot_nt(k_h, q_h)
            if masked:
                s = jnp.where(key <= qry, s, NEG_BIG)
            s_scr[u] = s
            m_prev = m_scr[h:h + 1, tile]
            m_new = jnp.maximum(m_prev, jnp.max(s, axis=0, keepdims=True))
            m_scr[h:h + 1, tile] = m_new
            stats.append((m_new, jnp.exp2(m_prev - m_new)))
        p_bf = [jnp.exp2(s_scr[u] - stats[u][0]).astype(BF16) for u in range(len(units))]
        ones = jnp.ones((SUM_ROWS, TK), BF16)
        for u, (r, kj, masked, h) in enumerate(units):
            tile = slice(r * TQ, (r + 1) * TQ)
            cols = slice(h * hd, (h + 1) * hd)
            alpha = stats[u][1]
            v_ext = jnp.concatenate([vt_ref[kj, cols, :], ones], axis=0)
            pv = _dot(v_ext, p_bf[u])
            acc_scr[cols, tile] = alpha * acc_scr[cols, tile] + pv[:hd, :]
            l_scr[h:h + 1, tile] = alpha * l_scr[h:h + 1, tile] + pv[hd:hd + 1, :]

    process(_diag_groups(step, "any"))

    def body(it, carry):
        process([(r, it * CHUNKS_PER_ITER + c, False)
                 for c in range(CHUNKS_PER_ITER) for r in range(Q_TILES)])
        return carry

    lax.fori_loop(0, step * Q_TILES // CHUNKS_PER_ITER, body, 0)

    for r in range(Q_TILES):
        tile = slice(r * TQ, (r + 1) * TQ)
        for h in range(heads):
            cols = slice(h * hd, (h + 1) * hd)
            o_t = acc_scr[cols, tile] / l_scr[h:h + 1, tile]
            o_ref[tile, cols] = o_t.T.astype(BF16)


def _mla_attn(qn, qp, kn, kpe, vt, batch, seq, cast_weights, cast_layers):
    t = qn.shape[0]
    steps = seq // TQ_STEP
    job_in, job_out, job_shapes = _cast_jobs(cast_weights, cast_layers, batch * steps,
                                             lambda b, i: b * steps + i)
    nch = seq // TK
    hv = MLA_HEADS * MLA_V
    kn3 = kn.reshape(batch, seq, -1)
    kpe3 = kpe.reshape(batch, seq, -1)
    vt4 = vt.reshape(batch, nch, hv, TK)
    qrow = lambda b, i: (b * steps + i, 0)
    perb = lambda b, i: (b, 0, 0)
    n_units = MLA_HEADS * MLA_GROUPS_PER_BLOCK
    return pl.pallas_call(
        functools.partial(_mla_attn_kernel, n_jobs=len(job_in)),
        grid=(batch, steps),
        in_specs=[pl.BlockSpec((TQ_STEP, qn.shape[1]), qrow),
                  pl.BlockSpec((TQ_STEP, qp.shape[1]), qrow),
                  pl.BlockSpec((None, seq, kn3.shape[2]), perb),
                  pl.BlockSpec((None, seq, kpe3.shape[2]), perb),
                  pl.BlockSpec((None, nch, hv, TK), lambda b, i: (b, 0, 0, 0))] + job_in,
        out_specs=[pl.BlockSpec((TQ_STEP, hv), qrow)] + job_out,
        out_shape=[jax.ShapeDtypeStruct((t, hv), BF16)] + job_shapes,
        scratch_shapes=[pltpu.VMEM((MLA_HEADS, TQ_STEP), F32), pltpu.VMEM((MLA_HEADS, TQ_STEP), F32),
                        pltpu.VMEM((hv, TQ_STEP), F32), pltpu.VMEM((n_units, TK, TQ), F32)],
        compiler_params=_params(("parallel", "arbitrary"), 48 * 1024 * 1024),
        name="mla_attn",
    )(qn, qp, kn3, kpe3, vt4, *cast_weights)


SB_SCALE = SB_HEAD_DIM ** -0.5 * LOG2E
SB_W = SB_HEADS * SB_HEAD_DIM


def _sb_proj_kernel(x_ref, g_ref, w_ref, qt_ref, k_ref, vt_ref, wk_scr, wqvt_scr):
    @pl.when(pl.program_id(0) == 0)
    def _():
        wk_scr[...] = w_ref[:, SB_W:2 * SB_W].astype(BF16)
        wqvt_scr[:SB_W, :] = w_ref[:, :SB_W].T.astype(BF16)
        wqvt_scr[SB_W:, :] = w_ref[:, 2 * SB_W:].T.astype(BF16)

    a = _rmsnorm(x_ref[...], g_ref[...]).astype(BF16)
    k_ref[...] = _dot(a, wk_scr[...]).astype(BF16)
    qv_t = _dot_nt(wqvt_scr[...], a)
    qt_ref[...] = (qv_t[:SB_W, :] * SB_SCALE).astype(BF16)
    for c in range(vt_ref.shape[0]):
        vt_ref[c] = qv_t[SB_W:, c * TK:(c + 1) * TK].astype(BF16)


def _sb_proj(x2, g, w_qkv):
    t, d = x2.shape
    tm = TM
    row = lambda i: (i, 0)
    once = pl.BlockSpec(w_qkv.shape, lambda i: (0, 0), pipeline_mode=pl.Buffered(1))
    return pl.pallas_call(
        _sb_proj_kernel,
        grid=(t // tm,),
        in_specs=[pl.BlockSpec((tm, d), row), _const_spec((1, d)), once],
        out_specs=[pl.BlockSpec((SB_W, tm), lambda i: (0, i)), pl.BlockSpec((tm, SB_W), row),
                   pl.BlockSpec((tm // TK, SB_W, TK), lambda i: (i, 0, 0))],
        out_shape=[jax.ShapeDtypeStruct((SB_W, t), BF16), jax.ShapeDtypeStruct((t, SB_W), BF16),
                   jax.ShapeDtypeStruct((t // TK, SB_W, TK), BF16)],
        scratch_shapes=[pltpu.VMEM((d, SB_W), BF16), pltpu.VMEM((2 * SB_W, d), BF16)],
        compiler_params=_params(("arbitrary",), 56 * 1024 * 1024),
        name="sb_proj",
    )(x2, g.reshape(1, d), w_qkv)


SOFTPLUS_LINEAR_FROM = 64.0
SB_DEAD_BITS = 160.0


def _sb_attn_kernel(q_ref, k_ref, vt_ref, *refs, n_jobs):
    o_ref = refs[n_jobs]
    _run_cast_jobs(refs[:n_jobs], refs[n_jobs + 1:2 * n_jobs + 1])
    carry_scr, acc_scr, lb_scr, top_scr, sp_scr, a_scr = refs[2 * n_jobs + 1:]
    step = pl.program_id(1)
    heads, hd = SB_HEADS, SB_HEAD_DIM
    row = lax.broadcasted_iota(jnp.int32, (TK, TK), 0)
    col = lax.broadcasted_iota(jnp.int32, (TK, TK), 1)
    suffix = jnp.where((col > row) | (col == 0), 1.0, 0.0).astype(BF16)
    first_row = lax.broadcasted_iota(jnp.int32, (SUM_ROWS, TQ), 0) == 0
    carry_scr[...] = jnp.zeros(carry_scr.shape, F32)
    acc_scr[...] = jnp.zeros(acc_scr.shape, F32)

    def process(groups):
        units = [(r, kj, masked, h) for r, kj, masked in groups for h in range(heads)]
        key = lax.broadcasted_iota(jnp.int32, (TK, TQ), 0)
        qry = lax.broadcasted_iota(jnp.int32, (TK, TQ), 1)
        for u, (r, kj, masked, h) in enumerate(units):
            koff = _key_offset(kj)
            tile = slice(r * TQ, (r + 1) * TQ)
            cols = slice(h * hd, (h + 1) * hd)
            z = _dot(k_ref[pl.ds(koff, TK), cols], q_ref[cols, tile])
            if masked:
                z = jnp.where(key < qry, z, NEG_BIG)
            sp = jnp.where(z > SOFTPLUS_LINEAR_FROM, z, jnp.log2(1.0 + jnp.exp2(z)))
            lb_scr[u] = z - sp
            top_scr[u] = sp[:SUM_ROWS, :]
            sp_scr[u] = sp[SUM_ROWS:, :].astype(BF16)
        for u, (r, kj, masked, h) in enumerate(units):
            tile = slice(r * TQ, (r + 1) * TQ)
            seen = carry_scr[h:h + 1, tile]
            top = top_scr[u]
            rhs = jnp.concatenate(
                [jnp.where(first_row, seen, top).astype(BF16), sp_scr[u]], axis=0)
            later = _dot(suffix, rhs)
            seen_as_summed = seen.astype(BF16).astype(F32)
            carry_scr[h:h + 1, tile] = seen + top[0:1, :] + (later[0:1, :] - seen_as_summed)
            a_scr[u] = jnp.exp2(lb_scr[u] - later).astype(BF16)
        for u, (r, kj, masked, h) in enumerate(units):
            tile = slice(r * TQ, (r + 1) * TQ)
            cols = slice(h * hd, (h + 1) * hd)
            acc_scr[cols, tile] = acc_scr[cols, tile] + _dot(vt_ref[kj, cols, :], a_scr[u])

    first = step * Q_TILES
    process([(r, first + r, True) for r in range(Q_TILES)])

    def more_to_do(state):
        i, floor = state
        return jnp.logical_and(i < first, floor < SB_DEAD_BITS)

    def body(state):
        i, _ = state
        process([(r, first + r - 1 - i, False) for r in range(Q_TILES)])
        return i + 1, jnp.min(carry_scr[...])

    i_end, floor = lax.while_loop(more_to_do, body, (jnp.int32(0), jnp.min(carry_scr[...])))

    for extra in range(1, Q_TILES):
        @pl.when(jnp.logical_and(i_end == first, floor < SB_DEAD_BITS))
        def _():
            process([(r, r - extra, False) for r in range(extra, Q_TILES)])

    for r in range(Q_TILES):
        tile = slice(r * TQ, (r + 1) * TQ)
        for h in range(heads):
            cols = slice(h * hd, (h + 1) * hd)
            o_ref[tile, cols] = acc_scr[cols, tile].T.astype(BF16)


def _sb_attn(q, k, vt, batch, seq, cast_weights, cast_layers):
    t = k.shape[0]
    steps = seq // TQ_STEP
    job_in, job_out, job_shapes = _cast_jobs(cast_weights, cast_layers, batch * steps,
                                             lambda b, i: b * steps + i)
    nch = seq // TK
    k3 = k.reshape(batch, seq, SB_W)
    vt4 = vt.reshape(batch, nch, SB_W, TK)
    qrow = lambda b, i: (b * steps + i, 0)
    n_units = SB_HEADS * SB_GROUPS_PER_BLOCK
    return pl.pallas_call(
        functools.partial(_sb_attn_kernel, n_jobs=len(job_in)),
        grid=(batch, steps),
        in_specs=[pl.BlockSpec((SB_W, TQ_STEP), lambda b, i: (0, b * steps + i)),
                  pl.BlockSpec((None, seq, SB_W), lambda b, i: (b, 0, 0)),
                  pl.BlockSpec((None, nch, SB_W, TK), lambda b, i: (b, 0, 0, 0))] + job_in,
        out_specs=[pl.BlockSpec((TQ_STEP, SB_W), qrow)] + job_out,
        out_shape=[jax.ShapeDtypeStruct((t, SB_W), BF16)] + job_shapes,
        scratch_shapes=[pltpu.VMEM((SB_HEADS, TQ_STEP), F32), pltpu.VMEM((SB_W, TQ_STEP), F32),
                        pltpu.VMEM((n_units, TK, TQ), F32),
                        pltpu.VMEM((n_units, SUM_ROWS, TQ), F32),
                        pltpu.VMEM((n_units, TK - SUM_ROWS, TQ), BF16),
                        pltpu.VMEM((n_units, TK, TQ), BF16)],
        compiler_params=_params(("parallel", "arbitrary"), 48 * 1024 * 1024),
        name="sb_attn",
    )(q, k3, vt4, *cast_weights)


MEM_SCALE = MEM_HEAD_DIM ** -0.5 * LOG2E
MEM_W = MEM_HEADS * MEM_HEAD_DIM


def _cross_kernel(h_ref, o_ref, wo_ref, g_ref, wq_ref, kv_ref, wxo_ref, out_ref):
    h1 = h_ref[...] + _dot(o_ref[...], wo_ref[...])
    n = _rmsnorm(h1, g_ref[...]).astype(BF16)
    q = (_dot(n, wq_ref[...]) * MEM_SCALE).astype(BF16)
    head_cols = [slice(hh * MEM_HEAD_DIM, (hh + 1) * MEM_HEAD_DIM) for hh in range(MEM_HEADS)]
    scores = [_dot_nt(q[:, c], kv_ref[:, c]) for c in head_cols]
    probs, denoms = [], []
    for s in scores:
        p = jnp.exp2(s - jnp.max(s, axis=-1, keepdims=True))
        denoms.append(jnp.sum(p, axis=-1, keepdims=True))
        probs.append(p.astype(BF16))
    outs = []
    for c, p, denom in zip(head_cols, probs, denoms):
        v_h = kv_ref[:, MEM_W + c.start:MEM_W + c.stop]
        outs.append((_dot(p, v_h) / denom).astype(BF16))
    o = jnp.concatenate(outs, axis=1)
    out_ref[...] = h1 + _dot(o, wxo_ref[...])


def _cross(h2, o, w_o, g, w_q, kv, w_xo, layer, seq):
    t, d = h2.shape
    tm = TM
    per_batch = seq // tm
    row = lambda i: (i, 0)
    return pl.pallas_call(
        _cross_kernel,
        grid=(t // tm,),
        in_specs=[pl.BlockSpec((tm, d), row), pl.BlockSpec((tm, o.shape[1]), row),
                  _const_spec(w_o.shape), _layer_spec(g, layer), _const_spec(w_q.shape),
                  pl.BlockSpec((None, None) + kv.shape[2:],
                               lambda i: (layer, i // per_batch, 0, 0)),
                  _const_spec(w_xo.shape)],
        out_specs=pl.BlockSpec((tm, d), row),
        out_shape=jax.ShapeDtypeStruct((t, d), F32),
        compiler_params=_params(("parallel",), 48 * 1024 * 1024),
        name="cross_attn",
    )(h2, o, w_o, g, w_q, kv, w_xo)


def _mlp_kernel(h_ref, g_ref, win_ref, wout_ref, gf_ref, out_ref, *, final_norm):
    h = h_ref[...]
    n = _rmsnorm(h, g_ref[...]).astype(BF16)
    acc = h
    d_ff = win_ref.shape[1]
    for c in range(d_ff // FF_CHUNK):
        cols = slice(c * FF_CHUNK, (c + 1) * FF_CHUNK)
        u = jnp.maximum(_dot(n, win_ref[:, cols]), 0.0)
        acc = acc + _dot((u * u).astype(BF16), wout_ref[cols, :])
    if final_norm:
        acc = _rmsnorm(acc, gf_ref[...])
    out_ref[...] = acc


def _mlp(h2, g, w_in, w_out, layer, g_final, final_norm):
    t, d = h2.shape
    tm = TM
    row = lambda i: (i, 0)
    return pl.pallas_call(
        functools.partial(_mlp_kernel, final_norm=final_norm),
        grid=(t // tm,),
        in_specs=[pl.BlockSpec((tm, d), row), _layer_spec(g, layer),
                  _const_spec(w_in.shape), _const_spec(w_out.shape), _const_spec((1, d))],
        out_specs=pl.BlockSpec((tm, d), row),
        out_shape=jax.ShapeDtypeStruct((t, d), F32),
        compiler_params=_params(("parallel",), 56 * 1024 * 1024),
        name="mlp",
    )(h2, g, w_in, w_out, g_final.reshape(1, d))


def kernel(x, mem, positions, norm_mix, norm_cross, norm_mem, norm_mlp, norm_final,
           mla_w_dkv, mla_g_q, mla_g_kv, mla_w_uq, mla_w_ukv, mla_w_o,
           sb_w_qkv, sb_w_o, xa_w_q, xa_w_kv, xa_w_o, mlp_w_in, mlp_w_out):
    batch, seq, d = x.shape
    depth = norm_mix.shape[0]
    assert depth == 2 and seq % TM == 0 and TM % TK == 0 and TQ == TK and seq % TQ_STEP == 0
    assert Q_TILES % CHUNKS_PER_ITER == 0
    h = x.reshape(batch * seq, d)
    cos, sin = _rope_tables(positions)
    mem_kv = _mem_kv(mem, norm_mem, xa_w_kv.astype(BF16))

    g_cross, g_mlp = _rows(norm_cross), _rows(norm_mlp)

    for i in range(depth):
        j = i // 2
        if i % 2 == 0:
            casts = ((mla_w_o, xa_w_q, xa_w_o, mlp_w_in, mlp_w_out), (j, i, i, i, i))
            qn, qp, kn, kpe, vt = _mla_proj(h, norm_mix[i], mla_w_dkv[j], mla_g_q[j],
                                            mla_g_kv[j], mla_w_uq[j], mla_w_ukv[j], cos, sin)
            o, w_o, w_q, w_xo, w_in, w_out = _mla_attn(qn, qp, kn, kpe, vt, batch, seq, *casts)
        else:
            casts = ((sb_w_o, xa_w_q, xa_w_o, mlp_w_in, mlp_w_out), (j, i, i, i, i))
            q, k, vt = _sb_proj(h, norm_mix[i], sb_w_qkv[j])
            o, w_o, w_q, w_xo, w_in, w_out = _sb_attn(q, k, vt, batch, seq, *casts)
        h = _cross(h, o, w_o, g_cross, w_q, mem_kv, w_xo, i, seq)
        h = _mlp(h, g_mlp, w_in, w_out, i, norm_final, final_norm=(i == depth - 1))
    return h.reshape(batch, seq, d)
```

```python
import functools
import math

import jax
import jax.numpy as jnp
from jax import lax
from jax.experimental import pallas as pl
from jax.experimental.pallas import tpu as pltpu

F32 = jnp.float32
BF16 = jnp.bfloat16

EPS = 1e-6
MLA_HEADS = 8
MLA_Q_LORA = 384
MLA_KV_LORA = 256
MLA_NOPE = 128
MLA_ROPE = 64
MLA_V = 128
ROPE_THETA = 10000.0
SB_HEADS = 8
SB_HEAD_DIM = 128
MEM_HEADS = 4
MEM_HEAD_DIM = 128
NEG_BIG = -1e30
LOG2E = math.log2(math.e)

LANES = 128
V7X_VMEM_BYTES = 64 * 1024 * 1024

TQ = 256
TK = 256
TM = 1024
FF_CHUNK = 1024


def _vmem_limit(nbytes):
    return int(min(max(nbytes, 32 * 1024 * 1024), V7X_VMEM_BYTES - 8 * 1024 * 1024))


def _params(semantics, vmem_bytes):
    return pltpu.CompilerParams(dimension_semantics=semantics,
                                vmem_limit_bytes=_vmem_limit(vmem_bytes))


def _const_spec(shape):
    nd = len(shape)
    return pl.BlockSpec(shape, lambda *_: (0,) * nd)


def _layer_spec(stacked, layer):
    rest = stacked.shape[1:]
    return pl.BlockSpec((None,) + rest, lambda *_: (layer,) + (0,) * len(rest))


def _rows(param):
    return param.reshape(param.shape[0], 1, param.shape[1])


def _cast_jobs(stacked_weights, layers, n_steps, slab_of):
    in_specs, out_specs, out_shapes = [], [], []
    for w, layer in zip(stacked_weights, layers):
        rows, cols = w.shape[1] // n_steps, w.shape[2]
        assert rows * n_steps == w.shape[1] and rows % 16 == 0
        in_specs.append(pl.BlockSpec((None, rows, cols),
                                     lambda *g, layer=layer: (layer, slab_of(*g), 0)))
        out_specs.append(pl.BlockSpec((rows, cols), lambda *g: (slab_of(*g), 0)))
        out_shapes.append(jax.ShapeDtypeStruct(w.shape[1:], BF16))
    return in_specs, out_specs, out_shapes


def _run_cast_jobs(src_refs, dst_refs):
    for src, dst in zip(src_refs, dst_refs):
        dst[...] = src[...].astype(BF16)


def _rmsnorm(x, g):
    return x * lax.rsqrt(jnp.mean(x * x, axis=-1, keepdims=True) + EPS) * g


def _dot(a, b):
    return jnp.dot(a, b, preferred_element_type=F32)


def _dot_nt(a, b):
    return lax.dot_general(a, b, (((1,), (1,)), ((), ())), preferred_element_type=F32)


ROPE_HALF = MLA_ROPE // 2
ROPE_PER_ROW = LANES // ROPE_HALF


def _exact_bf16_terms(x):
    hi = x.astype(BF16)
    rest = x - hi.astype(F32)
    mid = rest.astype(BF16)
    lo = (rest - mid.astype(F32)).astype(BF16)
    return hi, mid, lo


def _rope_kernel(pos_ref, freq_ref, cos_ref, sin_ref):
    ang = pos_ref[...].astype(F32) * freq_ref[...]
    src = lax.broadcasted_iota(jnp.int32, (LANES, LANES), 0)
    dst = lax.broadcasted_iota(jnp.int32, (LANES, LANES), 1)
    for table, out_ref in ((jnp.cos(ang), cos_ref), (jnp.sin(ang), sin_ref)):
        terms = _exact_bf16_terms(table)
        for j in range(ROPE_PER_ROW):
            spread = jnp.where(src == j * ROPE_HALF + (dst & (ROPE_HALF - 1)), 1.0, 0.0).astype(BF16)
            out_ref[j] = _dot(terms[0], spread) + _dot(terms[1], spread) + _dot(terms[2], spread)


def _rope_tables(positions):
    n_tok = positions.size
    inv_freq = ROPE_THETA ** (-jnp.arange(0, MLA_ROPE, 2, dtype=F32) / MLA_ROPE)
    rows = n_tok // ROPE_PER_ROW
    pos_rep = jnp.repeat(positions.reshape(ROPE_PER_ROW, rows).T, ROPE_HALF, axis=1)
    freq = jnp.tile(inv_freq, ROPE_PER_ROW).reshape(1, LANES)
    rb = 512
    out_spec = pl.BlockSpec((ROPE_PER_ROW, rb, LANES), lambda i: (0, i, 0))
    cos, sin = pl.pallas_call(
        _rope_kernel,
        grid=(rows // rb,),
        in_specs=[pl.BlockSpec((rb, LANES), lambda i: (i, 0)), _const_spec((1, LANES))],
        out_specs=[out_spec] * 2,
        out_shape=[jax.ShapeDtypeStruct((ROPE_PER_ROW, rows, LANES), F32)] * 2,
        compiler_params=_params(("parallel",), 0),
        name="rope_tables",
    )(pos_rep, freq)
    return cos.reshape(n_tok, LANES), sin.reshape(n_tok, LANES)


def _mem_kv_kernel(mem_ref, g_ref, w_ref, kv_ref):
    n = _rmsnorm(mem_ref[...], g_ref[...]).astype(BF16)
    kv_ref[...] = _dot(n, w_ref[...]).astype(BF16)


def _mem_kv(mem, norm_mem, w_kv):
    depth = norm_mem.shape[0]
    b, m, d = mem.shape
    n_out = w_kv.shape[-1]
    return pl.pallas_call(
        _mem_kv_kernel,
        grid=(depth, b),
        in_specs=[pl.BlockSpec((None, m, d), lambda l, i: (i, 0, 0)),
                  pl.BlockSpec((None, 1, d), lambda l, i: (l, 0, 0)),
                  pl.BlockSpec((None, d, n_out), lambda l, i: (l, 0, 0))],
        out_specs=pl.BlockSpec((None, None, m, n_out), lambda l, i: (l, i, 0, 0)),
        out_shape=jax.ShapeDtypeStruct((depth, b, m, n_out), BF16),
        compiler_params=_params(("parallel", "parallel"), 0),
        name="mem_kv",
    )(mem, norm_mem.reshape(depth, 1, d), w_kv)


MLA_QK_SCALE = (MLA_NOPE + MLA_ROPE) ** -0.5 * LOG2E
LAT_Q_END = MLA_Q_LORA
LAT_KV_END = MLA_Q_LORA + MLA_KV_LORA
LAT_EXT = LAT_KV_END + 2 * MLA_ROPE
Q_NOPE_W = MLA_HEADS * MLA_NOPE
Q_ROPE_W = MLA_HEADS * MLA_ROPE


def _mla_proj_kernel(x_ref, g_ref, wdkv_ref, gq_ref, gkv_ref, wuq_ref, wuk_ref, wvt_ref,
                     cos_ref, sin_ref, qn_ref, qp_ref, kn_ref, kpe_ref, vt_ref):
    a = _rmsnorm(x_ref[...], g_ref[...]).astype(BF16)
    lat = _dot(a, wdkv_ref[...])
    c_q = _rmsnorm(lat[:, :LAT_Q_END], gq_ref[...]).astype(BF16)
    c_kv = _rmsnorm(lat[:, LAT_Q_END:LAT_KV_END], gkv_ref[...]).astype(BF16)
    cos = cos_ref[...]
    sin = sin_ref[...]

    slab = lat[:, LAT_KV_END:LAT_EXT]
    rot = slab * cos + pltpu.roll(slab, MLA_ROPE, 1) * sin
    lane = lax.broadcasted_iota(jnp.int32, rot.shape, 1)
    k_even = jnp.where(lane < MLA_ROPE, rot, 0.0)
    k_odd = pltpu.roll(k_even, MLA_ROPE, 1)
    kpe_ref[...] = jnp.concatenate([k_even, k_odd], axis=1).astype(BF16)

    q = _dot(c_q, wuq_ref[...])
    qn_ref[...] = (q[:, :Q_NOPE_W] * MLA_QK_SCALE).astype(BF16)
    for s in range(Q_ROPE_W // LANES):
        lo = Q_NOPE_W + s * LANES
        r = q[:, lo:lo + LANES]
        r_partner = q[:, lo + Q_ROPE_W:lo + Q_ROPE_W + LANES]
        qp_ref[:, s * LANES:(s + 1) * LANES] = (
            (r * cos + r_partner * sin) * MLA_QK_SCALE).astype(BF16)

    kn_ref[...] = _dot(c_kv, wuk_ref[...]).astype(BF16)
    vt = _dot_nt(wvt_ref[...], c_kv)
    for c in range(vt_ref.shape[0]):
        vt_ref[c] = vt[:, c * TK:(c + 1) * TK].astype(BF16)


def _mla_proj(x2, g, w_dkv, g_q, g_kv, w_uq, w_ukv, cos, sin):
    t, d = x2.shape
    h = MLA_HEADS
    kpe0 = LAT_KV_END
    half = MLA_ROPE // 2
    wdkv = jnp.concatenate(
        [w_dkv, -w_dkv[:, kpe0 + half:kpe0 + MLA_ROPE], w_dkv[:, kpe0:kpe0 + half]],
        axis=1).astype(BF16)
    uq = w_uq.reshape(MLA_Q_LORA, h, MLA_NOPE + MLA_ROPE)
    uq_rope = uq[:, :, MLA_NOPE:]
    uq_partner = jnp.concatenate([-uq_rope[:, :, half:], uq_rope[:, :, :half]], axis=-1)
    wuq = jnp.concatenate(
        [uq[:, :, :MLA_NOPE].reshape(MLA_Q_LORA, Q_NOPE_W),
         uq_rope.reshape(MLA_Q_LORA, Q_ROPE_W),
         uq_partner.reshape(MLA_Q_LORA, Q_ROPE_W)], axis=1).astype(BF16)
    ukv = w_ukv.reshape(MLA_KV_LORA, h, MLA_NOPE + MLA_V)
    wuk = ukv[:, :, :MLA_NOPE].reshape(MLA_KV_LORA, h * MLA_NOPE).astype(BF16)
    wvt = ukv[:, :, MLA_NOPE:].reshape(MLA_KV_LORA, h * MLA_V).T.astype(BF16)

    tm = TM
    n_chunks = t // TK
    row = lambda i: (i, 0)
    out_shapes = [
        jax.ShapeDtypeStruct((t, Q_NOPE_W), BF16),
        jax.ShapeDtypeStruct((t, Q_ROPE_W), BF16),
        jax.ShapeDtypeStruct((t, h * MLA_NOPE), BF16),
        jax.ShapeDtypeStruct((t, 2 * LANES), BF16),
        jax.ShapeDtypeStruct((n_chunks, h * MLA_V, TK), BF16),
    ]
    return pl.pallas_call(
        _mla_proj_kernel,
        grid=(t // tm,),
        in_specs=[pl.BlockSpec((tm, d), row), _const_spec((1, d)),
                  _const_spec(wdkv.shape), _const_spec((1, MLA_Q_LORA)),
                  _const_spec((1, MLA_KV_LORA)), _const_spec(wuq.shape),
                  _const_spec(wuk.shape), _const_spec(wvt.shape),
                  pl.BlockSpec((tm, LANES), row), pl.BlockSpec((tm, LANES), row)],
        out_specs=[pl.BlockSpec((tm, Q_NOPE_W), row), pl.BlockSpec((tm, Q_ROPE_W), row),
                   pl.BlockSpec((tm, h * MLA_NOPE), row), pl.BlockSpec((tm, 2 * LANES), row),
                   pl.BlockSpec((tm // TK, h * MLA_V, TK), lambda i: (i, 0, 0))],
        out_shape=out_shapes,
        compiler_params=_params(("parallel",), 48 * 1024 * 1024),
        name="mla_proj",
    )(x2, g.reshape(1, d), wdkv, g_q.reshape(1, -1), g_kv.reshape(1, -1), wuq, wuk, wvt,
      cos, sin)


Q_TILES = 2
SUM_ROWS = 16
TQ_STEP = Q_TILES * TQ
CHUNKS_PER_ITER = 2
DIAG_GROUPS = Q_TILES * (Q_TILES + 1) // 2
MLA_GROUPS_PER_BLOCK = max(DIAG_GROUPS, Q_TILES * CHUNKS_PER_ITER)
SB_GROUPS_PER_BLOCK = Q_TILES


def _diag_groups(step):
    return [(r, step * Q_TILES + c, c == r) for r in range(Q_TILES) for c in range(r + 1)]


def _key_offset(kj):
    off = kj * TK
    return off if isinstance(off, int) else pl.multiple_of(off, TK)


def _mla_attn_kernel(qn_ref, qp_ref, kn_ref, kpe_ref, vt_ref, *refs, n_jobs):
    o_ref = refs[n_jobs]
    _run_cast_jobs(refs[:n_jobs], refs[n_jobs + 1:2 * n_jobs + 1])
    m_scr, l_scr, acc_scr, s_scr = refs[2 * n_jobs + 1:]
    step = pl.program_id(1)
    heads, hd = MLA_HEADS, MLA_NOPE
    m_scr[...] = jnp.full(m_scr.shape, NEG_BIG, F32)
    l_scr[...] = jnp.zeros(l_scr.shape, F32)
    acc_scr[...] = jnp.zeros(acc_scr.shape, F32)

    def process(groups):
        units = [(r, kj, masked, h) for r, kj, masked in groups for h in range(heads)]
        key = lax.broadcasted_iota(jnp.int32, (TK, TQ), 0)
        qry = lax.broadcasted_iota(jnp.int32, (TK, TQ), 1)
        stats = []
        for u, (r, kj, masked, h) in enumerate(units):
            koff = _key_offset(kj)
            tile = slice(r * TQ, (r + 1) * TQ)
            cols = slice(h * hd, (h + 1) * hd)
            pair = slice((h // 2) * LANES, (h // 2 + 1) * LANES)
            par = slice((h % 2) * LANES, (h % 2 + 1) * LANES)
            q_h = jnp.concatenate([qn_ref[tile, cols], qp_ref[tile, pair]], axis=1)
            k_h = jnp.concatenate([kn_ref[pl.ds(koff, TK), cols],
                                   kpe_ref[pl.ds(koff, TK), par]], axis=1)
            s = _dot_nt(k_h, q_h)
            if masked:
                s = jnp.where(key <= qry, s, NEG_BIG)
            s_scr[u] = s
            m_prev = m_scr[h:h + 1, tile]
            m_new = jnp.maximum(m_prev, jnp.max(s, axis=0, keepdims=True))
            m_scr[h:h + 1, tile] = m_new
            stats.append((m_new, jnp.exp2(m_prev - m_new)))
        p_bf = [jnp.exp2(s_scr[u] - stats[u][0]).astype(BF16) for u in range(len(units))]
        ones = jnp.ones((SUM_ROWS, TK), BF16)
        for u, (r, kj, masked, h) in enumerate(units):
            tile = slice(r * TQ, (r + 1) * TQ)
            cols = slice(h * hd, (h + 1) * hd)
            alpha = stats[u][1]
            v_ext = jnp.concatenate([vt_ref[kj, cols, :], ones], axis=0)
            pv = _dot(v_ext, p_bf[u])
            acc_scr[cols, tile] = alpha * acc_scr[cols, tile] + pv[:hd, :]
            l_scr[h:h + 1, tile] = alpha * l_scr[h:h + 1, tile] + pv[hd:hd + 1, :]

    process(_diag_groups(step))

    def body(it, carry):
        process([(r, it * CHUNKS_PER_ITER + c, False)
                 for c in range(CHUNKS_PER_ITER) for r in range(Q_TILES)])
        return carry

    lax.fori_loop(0, step * Q_TILES // CHUNKS_PER_ITER, body, 0)

    for r in range(Q_TILES):
        tile = slice(r * TQ, (r + 1) * TQ)
        for h in range(heads):
            cols = slice(h * hd, (h + 1) * hd)
            o_t = acc_scr[cols, tile] / l_scr[h:h + 1, tile]
            o_ref[tile, cols] = o_t.T.astype(BF16)


def _mla_attn(qn, qp, kn, kpe, vt, batch, seq, cast_weights, cast_layers):
    t = qn.shape[0]
    steps = seq // TQ_STEP
    job_in, job_out, job_shapes = _cast_jobs(cast_weights, cast_layers, batch * steps,
                                             lambda b, i: b * steps + i)
    nch = seq // TK
    hv = MLA_HEADS * MLA_V
    kn3 = kn.reshape(batch, seq, -1)
    kpe3 = kpe.reshape(batch, seq, -1)
    vt4 = vt.reshape(batch, nch, hv, TK)
    qrow = lambda b, i: (b * steps + i, 0)
    perb = lambda b, i: (b, 0, 0)
    n_units = MLA_HEADS * MLA_GROUPS_PER_BLOCK
    return pl.pallas_call(
        functools.partial(_mla_attn_kernel, n_jobs=len(job_in)),
        grid=(batch, steps),
        in_specs=[pl.BlockSpec((TQ_STEP, qn.shape[1]), qrow),
                  pl.BlockSpec((TQ_STEP, qp.shape[1]), qrow),
                  pl.BlockSpec((None, seq, kn3.shape[2]), perb),
                  pl.BlockSpec((None, seq, kpe3.shape[2]), perb),
                  pl.BlockSpec((None, nch, hv, TK), lambda b, i: (b, 0, 0, 0))] + job_in,
        out_specs=[pl.BlockSpec((TQ_STEP, hv), qrow)] + job_out,
        out_shape=[jax.ShapeDtypeStruct((t, hv), BF16)] + job_shapes,
        scratch_shapes=[pltpu.VMEM((MLA_HEADS, TQ_STEP), F32), pltpu.VMEM((MLA_HEADS, TQ_STEP), F32),
                        pltpu.VMEM((hv, TQ_STEP), F32), pltpu.VMEM((n_units, TK, TQ), F32)],
        compiler_params=_params(("parallel", "arbitrary"), 48 * 1024 * 1024),
        name="mla_attn",
    )(qn, qp, kn3, kpe3, vt4, *cast_weights)


SB_SCALE = SB_HEAD_DIM ** -0.5 * LOG2E
SB_W = SB_HEADS * SB_HEAD_DIM


def _sb_proj_kernel(x_ref, g_ref, w_ref, qt_ref, k_ref, vt_ref, wk_scr, wqvt_scr):
    @pl.when(pl.program_id(0) == 0)
    def _():
        wk_scr[...] = w_ref[:, SB_W:2 * SB_W].astype(BF16)
        wqvt_scr[:SB_W, :] = w_ref[:, :SB_W].T.astype(BF16)
        wqvt_scr[SB_W:, :] = w_ref[:, 2 * SB_W:].T.astype(BF16)

    a = _rmsnorm(x_ref[...], g_ref[...]).astype(BF16)
    k_ref[...] = _dot(a, wk_scr[...]).astype(BF16)
    qv_t = _dot_nt(wqvt_scr[...], a)
    qt_ref[...] = (qv_t[:SB_W, :] * SB_SCALE).astype(BF16)
    for c in range(vt_ref.shape[0]):
        vt_ref[c] = qv_t[SB_W:, c * TK:(c + 1) * TK].astype(BF16)


def _sb_proj(x2, g, w_qkv):
    t, d = x2.shape
    tm = TM
    row = lambda i: (i, 0)
    once = pl.BlockSpec(w_qkv.shape, lambda i: (0, 0), pipeline_mode=pl.Buffered(1))
    return pl.pallas_call(
        _sb_proj_kernel,
        grid=(t // tm,),
        in_specs=[pl.BlockSpec((tm, d), row), _const_spec((1, d)), once],
        out_specs=[pl.BlockSpec((SB_W, tm), lambda i: (0, i)), pl.BlockSpec((tm, SB_W), row),
                   pl.BlockSpec((tm // TK, SB_W, TK), lambda i: (i, 0, 0))],
        out_shape=[jax.ShapeDtypeStruct((SB_W, t), BF16), jax.ShapeDtypeStruct((t, SB_W), BF16),
                   jax.ShapeDtypeStruct((t // TK, SB_W, TK), BF16)],
        scratch_shapes=[pltpu.VMEM((d, SB_W), BF16), pltpu.VMEM((2 * SB_W, d), BF16)],
        compiler_params=_params(("arbitrary",), 56 * 1024 * 1024),
        name="sb_proj",
    )(x2, g.reshape(1, d), w_qkv)


SOFTPLUS_LINEAR_FROM = 64.0
SB_DEAD_BITS = 160.0


def _sb_attn_kernel(q_ref, k_ref, vt_ref, *refs, n_jobs):
    o_ref = refs[n_jobs]
    _run_cast_jobs(refs[:n_jobs], refs[n_jobs + 1:2 * n_jobs + 1])
    carry_scr, acc_scr, lb_scr, top_scr, sp_scr, a_scr = refs[2 * n_jobs + 1:]
    step = pl.program_id(1)
    heads, hd = SB_HEADS, SB_HEAD_DIM
    row = lax.broadcasted_iota(jnp.int32, (TK, TK), 0)
    col = lax.broadcasted_iota(jnp.int32, (TK, TK), 1)
    suffix = jnp.where((col > row) | (col == 0), 1.0, 0.0).astype(BF16)
    first_row = lax.broadcasted_iota(jnp.int32, (SUM_ROWS, TQ), 0) == 0
    carry_scr[...] = jnp.zeros(carry_scr.shape, F32)
    acc_scr[...] = jnp.zeros(acc_scr.shape, F32)

    def process(groups):
        units = [(r, kj, masked, h) for r, kj, masked in groups for h in range(heads)]
        key = lax.broadcasted_iota(jnp.int32, (TK, TQ), 0)
        qry = lax.broadcasted_iota(jnp.int32, (TK, TQ), 1)
        for u, (r, kj, masked, h) in enumerate(units):
            koff = _key_offset(kj)
            tile = slice(r * TQ, (r + 1) * TQ)
            cols = slice(h * hd, (h + 1) * hd)
            z = _dot(k_ref[pl.ds(koff, TK), cols], q_ref[cols, tile])
            if masked:
                z = jnp.where(key < qry, z, NEG_BIG)
            sp = jnp.where(z > SOFTPLUS_LINEAR_FROM, z, jnp.log2(1.0 + jnp.exp2(z)))
            lb_scr[u] = z - sp
            top_scr[u] = sp[:SUM_ROWS, :]
            sp_scr[u] = sp[SUM_ROWS:, :].astype(BF16)
        for u, (r, kj, masked, h) in enumerate(units):
            tile = slice(r * TQ, (r + 1) * TQ)
            seen = carry_scr[h:h + 1, tile]
            top = top_scr[u]
            rhs = jnp.concatenate(
                [jnp.where(first_row, seen, top).astype(BF16), sp_scr[u]], axis=0)
            later = _dot(suffix, rhs)
            seen_as_summed = seen.astype(BF16).astype(F32)
            carry_scr[h:h + 1, tile] = seen + top[0:1, :] + (later[0:1, :] - seen_as_summed)
            a_scr[u] = jnp.exp2(lb_scr[u] - later).astype(BF16)
        for u, (r, kj, masked, h) in enumerate(units):
            tile = slice(r * TQ, (r + 1) * TQ)
            cols = slice(h * hd, (h + 1) * hd)
            acc_scr[cols, tile] = acc_scr[cols, tile] + _dot(vt_ref[kj, cols, :], a_scr[u])

    first = step * Q_TILES
    process([(r, first + r, True) for r in range(Q_TILES)])

    def more_to_do(state):
        i, floor = state
        return jnp.logical_and(i < first, floor < SB_DEAD_BITS)

    def body(state):
        i, _ = state
        process([(r, first + r - 1 - i, False) for r in range(Q_TILES)])
        return i + 1, jnp.min(carry_scr[...])

    i_end, floor = lax.while_loop(more_to_do, body, (jnp.int32(0), jnp.min(carry_scr[...])))

    for extra in range(1, Q_TILES):
        @pl.when(jnp.logical_and(i_end == first, floor < SB_DEAD_BITS))
        def _():
            process([(r, r - extra, False) for r in range(extra, Q_TILES)])

    for r in range(Q_TILES):
        tile = slice(r * TQ, (r + 1) * TQ)
        for h in range(heads):
            cols = slice(h * hd, (h + 1) * hd)
            o_ref[tile, cols] = acc_scr[cols, tile].T.astype(BF16)


def _sb_attn(q, k, vt, batch, seq, cast_weights, cast_layers):
    t = k.shape[0]
    steps = seq // TQ_STEP
    job_in, job_out, job_shapes = _cast_jobs(cast_weights, cast_layers, batch * steps,
                                             lambda b, i: b * steps + i)
    nch = seq // TK
    k3 = k.reshape(batch, seq, SB_W)
    vt4 = vt.reshape(batch, nch, SB_W, TK)
    qrow = lambda b, i: (b * steps + i, 0)
    n_units = SB_HEADS * SB_GROUPS_PER_BLOCK
    return pl.pallas_call(
        functools.partial(_sb_attn_kernel, n_jobs=len(job_in)),
        grid=(batch, steps),
        in_specs=[pl.BlockSpec((SB_W, TQ_STEP), lambda b, i: (0, b * steps + i)),
                  pl.BlockSpec((None, seq, SB_W), lambda b, i: (b, 0, 0)),
                  pl.BlockSpec((None, nch, SB_W, TK), lambda b, i: (b, 0, 0, 0))] + job_in,
        out_specs=[pl.BlockSpec((TQ_STEP, SB_W), qrow)] + job_out,
        out_shape=[jax.ShapeDtypeStruct((t, SB_W), BF16)] + job_shapes,
        scratch_shapes=[pltpu.VMEM((SB_HEADS, TQ_STEP), F32), pltpu.VMEM((SB_W, TQ_STEP), F32),
                        pltpu.VMEM((n_units, TK, TQ), F32),
                        pltpu.VMEM((n_units, SUM_ROWS, TQ), F32),
                        pltpu.VMEM((n_units, TK - SUM_ROWS, TQ), BF16),
                        pltpu.VMEM((n_units, TK, TQ), BF16)],
        compiler_params=_params(("parallel", "arbitrary"), 48 * 1024 * 1024),
        name="sb_attn",
    )(q, k3, vt4, *cast_weights)


MEM_SCALE = MEM_HEAD_DIM ** -0.5 * LOG2E
MEM_W = MEM_HEADS * MEM_HEAD_DIM


def _cross_kernel(h_ref, o_ref, wo_ref, g_ref, wq_ref, kv_ref, wxo_ref, out_ref):
    h1 = h_ref[...] + _dot(o_ref[...], wo_ref[...])
    n = _rmsnorm(h1, g_ref[...]).astype(BF16)
    q = (_dot(n, wq_ref[...]) * MEM_SCALE).astype(BF16)
    head_cols = [slice(hh * MEM_HEAD_DIM, (hh + 1) * MEM_HEAD_DIM) for hh in range(MEM_HEADS)]
    scores = [_dot_nt(q[:, c], kv_ref[:, c]) for c in head_cols]
    probs, denoms = [], []
    for s in scores:
        p = jnp.exp2(s - jnp.max(s, axis=-1, keepdims=True))
        denoms.append(jnp.sum(p, axis=-1, keepdims=True))
        probs.append(p.astype(BF16))
    outs = []
    for c, p, denom in zip(head_cols, probs, denoms):
        v_h = kv_ref[:, MEM_W + c.start:MEM_W + c.stop]
        outs.append((_dot(p, v_h) / denom).astype(BF16))
    o = jnp.concatenate(outs, axis=1)
    out_ref[...] = h1 + _dot(o, wxo_ref[...])


def _cross(h2, o, w_o, g, w_q, kv, w_xo, layer, seq):
    t, d = h2.shape
    tm = TM
    per_batch = seq // tm
    row = lambda i: (i, 0)
    return pl.pallas_call(
        _cross_kernel,
        grid=(t // tm,),
        in_specs=[pl.BlockSpec((tm, d), row), pl.BlockSpec((tm, o.shape[1]), row),
                  _const_spec(w_o.shape), _layer_spec(g, layer), _const_spec(w_q.shape),
                  pl.BlockSpec((None, None) + kv.shape[2:],
                               lambda i: (layer, i // per_batch, 0, 0)),
                  _const_spec(w_xo.shape)],
        out_specs=pl.BlockSpec((tm, d), row),
        out_shape=jax.ShapeDtypeStruct((t, d), F32),
        compiler_params=_params(("parallel",), 48 * 1024 * 1024),
        name="cross_attn",
    )(h2, o, w_o, g, w_q, kv, w_xo)


def _mlp_kernel(h_ref, g_ref, win_ref, wout_ref, gf_ref, out_ref, *, final_norm):
    h = h_ref[...]
    n = _rmsnorm(h, g_ref[...]).astype(BF16)
    acc = h
    d_ff = win_ref.shape[1]
    for c in range(d_ff // FF_CHUNK):
        cols = slice(c * FF_CHUNK, (c + 1) * FF_CHUNK)
        u = jnp.maximum(_dot(n, win_ref[:, cols]), 0.0)
        acc = acc + _dot((u * u).astype(BF16), wout_ref[cols, :])
    if final_norm:
        acc = _rmsnorm(acc, gf_ref[...])
    out_ref[...] = acc


def _mlp(h2, g, w_in, w_out, layer, g_final, final_norm):
    t, d = h2.shape
    tm = TM
    row = lambda i: (i, 0)
    return pl.pallas_call(
        functools.partial(_mlp_kernel, final_norm=final_norm),
        grid=(t // tm,),
        in_specs=[pl.BlockSpec((tm, d), row), _layer_spec(g, layer),
                  _const_spec(w_in.shape), _const_spec(w_out.shape), _const_spec((1, d))],
        out_specs=pl.BlockSpec((tm, d), row),
        out_shape=jax.ShapeDtypeStruct((t, d), F32),
        compiler_params=_params(("parallel",), 56 * 1024 * 1024),
        name="mlp",
    )(h2, g, w_in, w_out, g_final.reshape(1, d))


def kernel(x, mem, positions, norm_mix, norm_cross, norm_mem, norm_mlp, norm_final,
           mla_w_dkv, mla_g_q, mla_g_kv, mla_w_uq, mla_w_ukv, mla_w_o,
           sb_w_qkv, sb_w_o, xa_w_q, xa_w_kv, xa_w_o, mlp_w_in, mlp_w_out):
    batch, seq, d = x.shape
    depth = norm_mix.shape[0]
    assert depth == 2 and seq % TM == 0 and TM % TK == 0 and TQ == TK and seq % TQ_STEP == 0
    assert Q_TILES % CHUNKS_PER_ITER == 0
    h = x.reshape(batch * seq, d)
    cos, sin = _rope_tables(positions)
    mem_kv = _mem_kv(mem, norm_mem, xa_w_kv.astype(BF16))

    g_cross, g_mlp = _rows(norm_cross), _rows(norm_mlp)

    for i in range(depth):
        j = i // 2
        if i % 2 == 0:
            casts = ((mla_w_o, xa_w_q, xa_w_o, mlp_w_in, mlp_w_out), (j, i, i, i, i))
            qn, qp, kn, kpe, vt = _mla_proj(h, norm_mix[i], mla_w_dkv[j], mla_g_q[j],
                                            mla_g_kv[j], mla_w_uq[j], mla_w_ukv[j], cos, sin)
            o, w_o, w_q, w_xo, w_in, w_out = _mla_attn(qn, qp, kn, kpe, vt, batch, seq, *casts)
        else:
            casts = ((sb_w_o, xa_w_q, xa_w_o, mlp_w_in, mlp_w_out), (j, i, i, i, i))
            q, k, vt = _sb_proj(h, norm_mix[i], sb_w_qkv[j])
            o, w_o, w_q, w_xo, w_in, w_out = _sb_attn(q, k, vt, batch, seq, *casts)
        h = _cross(h, o, w_o, g_cross, w_q, mem_kv, w_xo, i, seq)
        h = _mlp(h, g_mlp, w_in, w_out, i, norm_final, final_norm=(i == depth - 1))
    return h.reshape(batch, seq, d)
```
